```python
import math
import jax, jax.numpy as jnp
from jax import lax
import numpy as np

D_MODEL = 2048
BATCH = 4
SEQ = 2048
DEPTH = 1

HEAD_DIM = 128
N_HEADS = D_MODEL // HEAD_DIM
N_KV_HEADS = 4
ROPE_DIM = HEAD_DIM // 4
ROPE_THETA = 500000.0
N_IDX_HEADS = 16
IDX_DIM = 64
IDX_ROPE_DIM = IDX_DIM // 4
TOPK_MAX = 256
Q_BLOCK = 128
D_RNN = (4 * D_MODEL // 3) // 256 * 256
LRU_BLOCKS = 16
LRU_BLOCK = D_RNN // LRU_BLOCKS
CONV_WIDTH = 4
LRU_C = 8.0
D_FF = (8 * D_MODEL // 3 + 255) // 256 * 256
LN_EPS = 1e-5
DN_ALPHA = (2.0 * DEPTH) ** 0.25
DN_BETA = (8.0 * DEPTH) ** -0.25

IN_WIDTHS = (N_HEADS * HEAD_DIM, N_KV_HEADS * HEAD_DIM, N_KV_HEADS * HEAD_DIM,
             N_IDX_HEADS * IDX_DIM, IDX_DIM, N_IDX_HEADS, D_RNN, D_RNN, 2 * D_MODEL)
IN_TOTAL = sum(IN_WIDTHS)

kernel_name = "hybrid_dsa_rglru_macaron_deepnorm"


def layer_norm(x, g, b):
    xf = x.astype(jnp.float32)
    mu = jnp.mean(xf, axis=-1, keepdims=True)
    var = jnp.mean(jnp.square(xf - mu), axis=-1, keepdims=True)
    y = (xf - mu) * lax.rsqrt(var + LN_EPS) * g.astype(jnp.float32) + b.astype(jnp.float32)
    return y.astype(x.dtype)


def swiglu(h, w_in, w_out):
    a, b = jnp.split(h @ w_in, 2, axis=-1)
    return (jax.nn.silu(a) * b) @ w_out


def rope(t, positions, rot_dim):
    half = rot_dim // 2
    inv_freq = jnp.power(ROPE_THETA, -(jnp.arange(half, dtype=jnp.float32) * 2.0 / rot_dim))
    ang = positions.astype(jnp.float32)[..., None] * inv_freq
    cos = jnp.cos(ang)[:, :, None, :].astype(t.dtype)
    sin = jnp.sin(ang)[:, :, None, :].astype(t.dtype)
    t1 = t[..., :half]
    t2 = t[..., half:rot_dim]
    return jnp.concatenate([t1 * cos - t2 * sin, t2 * cos + t1 * sin, t[..., rot_dim:]], axis=-1)


def split_cols(p):
    offs = []
    acc = 0
    for w in IN_WIDTHS[:-1]:
        acc += w
        offs.append(acc)
    return jnp.split(p, offs, axis=-1)


def dsa_attention(q, k, v, qi, ki, wi):
    B, S = q.shape[0], q.shape[1]
    topk = min(TOPK_MAX, S // 4)
    nb = S // Q_BLOCK
    G = N_HEADS // N_KV_HEADS
    qg = q.reshape(B, S, N_KV_HEADS, G, HEAD_DIM)
    key_pos = jnp.arange(S)
    scale = HEAD_DIM ** -0.5
    gather = jax.vmap(lambda t, i: t[i])

    def blocks(a):
        return a.reshape((B, nb, Q_BLOCK) + a.shape[2:]).swapaxes(0, 1)

    def one_block(args):
        start, qb, qib, wib = args
        qpos = start + jnp.arange(Q_BLOCK)
        causal = key_pos[None, :] <= qpos[:, None]
        logits = jnp.einsum('bqhd,bsd->bqhs', qib, ki).astype(jnp.float32)
        score = jnp.einsum('bqhs,bqh->bqs', jax.nn.relu(logits), wib.astype(jnp.float32))
        score = jnp.where(causal[None], score, -jnp.inf)
        _, idx = lax.top_k(score, topk)
        valid = idx <= qpos[None, :, None]
        k_sel = gather(k, idx)
        v_sel = gather(v, idx)
        s = jnp.einsum('bqhgd,bqkhd->bqhgk', qb, k_sel).astype(jnp.float32) * scale
        s = jnp.where(valid[:, :, None, None, :], s, -jnp.inf)
        p = jax.nn.softmax(s, axis=-1).astype(v.dtype)
        o = jnp.einsum('bqhgk,bqkhd->bqhgd', p, v_sel)
        return o.reshape(B, Q_BLOCK, N_HEADS * HEAD_DIM)

    starts = jnp.arange(nb) * Q_BLOCK
    out = lax.map(one_block, (starts, blocks(qg), blocks(qi), blocks(wi)))
    return out.swapaxes(0, 1).reshape(B, S, N_HEADS * HEAD_DIM)


def rglru_branch(rx, rg, conv_w, conv_b, wa, ba, wx, bx, lam):
    B, S = rx.shape[0], rx.shape[1]
    xc = lax.conv_general_dilated(
        rx, conv_w[:, None, :], window_strides=(1,), padding=[(CONV_WIDTH - 1, 0)],
        dimension_numbers=('NWC', 'WIO', 'NWC'), feature_group_count=D_RNN) + conv_b
    xb = xc.reshape(B, S, LRU_BLOCKS, LRU_BLOCK)
    r = jax.nn.sigmoid(jnp.einsum('btnc,ncd->btnd', xb, wa).reshape(B, S, D_RNN) + ba)
    i = jax.nn.sigmoid(jnp.einsum('btnc,ncd->btnd', xb, wx).reshape(B, S, D_RNN) + bx)
    log_a = -LRU_C * r.astype(jnp.float32) * jax.nn.softplus(-lam.astype(jnp.float32))
    a = jnp.exp(log_a)
    b = jnp.sqrt(-jnp.expm1(2.0 * log_a)) * (i * xc).astype(jnp.float32)

    def combine(left, right):
        a1, b1 = left
        a2, b2 = right
        return a1 * a2, a2 * b1 + b2

    _, hseq = lax.associative_scan(combine, (a, b), axis=1)
    return hseq.astype(rx.dtype) * jax.nn.gelu(rg)


def setup_inputs(seed: int = 0) -> dict:
    key = jax.random.key(seed)
    ks = jax.random.split(key, 24)
    f32 = jnp.float32
    L = DEPTH

    def nrm(k, shape, fan_in, mult=1.0):
        return jax.random.normal(k, shape, f32) * (fan_in ** -0.5) * mult

    def gain(k):
        return 1.0 + 0.02 * jax.random.normal(k, (L, D_MODEL), f32)

    def bias(k, n):
        return 0.02 * jax.random.normal(k, (L, n), f32)

    u = jax.random.uniform(ks[13], (L, D_RNN), f32, minval=0.9, maxval=0.999)
    s = u ** (1.0 / LRU_C)
    lru_lambda = jnp.log(s) - jnp.log1p(-s)
    return {
        "x": jax.random.normal(ks[0], (BATCH, SEQ, D_MODEL), f32),
        "positions": jnp.tile(jnp.arange(SEQ, dtype=jnp.int32)[None, :], (BATCH, 1)),
        "ffn1_w_in": nrm(ks[1], (L, D_MODEL, 2 * D_FF), D_MODEL),
        "ffn1_w_out": nrm(ks[2], (L, D_FF, D_MODEL), D_FF, DN_BETA),
        "ln1_g": gain(ks[3]),
        "ln1_b": bias(ks[4], D_MODEL),
        "w_in": nrm(ks[5], (L, D_MODEL, IN_TOTAL), D_MODEL),
        "conv_w": nrm(ks[6], (L, CONV_WIDTH, D_RNN), CONV_WIDTH),
        "conv_b": bias(ks[7], D_RNN),
        "lru_wa": nrm(ks[8], (L, LRU_BLOCKS, LRU_BLOCK, LRU_BLOCK), LRU_BLOCK),
        "lru_ba": bias(ks[9], D_RNN),
        "lru_wx": nrm(ks[10], (L, LRU_BLOCKS, LRU_BLOCK, LRU_BLOCK), LRU_BLOCK),
        "lru_bx": bias(ks[11], D_RNN),
        "lru_lambda": lru_lambda,
        "w_attn_branch": nrm(ks[12], (L, N_HEADS * HEAD_DIM, D_MODEL), N_HEADS * HEAD_DIM),
        "w_rnn_branch": nrm(ks[14], (L, D_RNN, D_MODEL), D_RNN),
        "w_out": nrm(ks[15], (L, D_MODEL, D_MODEL), D_MODEL, DN_BETA),
        "ln2_g": gain(ks[16]),
        "ln2_b": bias(ks[17], D_MODEL),
        "ffn2_w_in": nrm(ks[18], (L, D_MODEL, 2 * D_FF), D_MODEL),
        "ffn2_w_out": nrm(ks[19], (L, D_FF, D_MODEL), D_FF, DN_BETA),
        "ln3_g": gain(ks[20]),
        "ln3_b": bias(ks[21], D_MODEL),
    }


def reference(x, positions, ffn1_w_in, ffn1_w_out, ln1_g, ln1_b, w_in, conv_w, conv_b,
              lru_wa, lru_ba, lru_wx, lru_bx, lru_lambda, w_attn_branch, w_rnn_branch, w_out,
              ln2_g, ln2_b, ffn2_w_in, ffn2_w_out, ln3_g, ln3_b):
    B, S, _ = x.shape
    h = x
    for l in range(DEPTH):
        h = layer_norm(DN_ALPHA * h + 0.5 * swiglu(h, ffn1_w_in[l], ffn1_w_out[l]), ln1_g[l], ln1_b[l])

        q, k, v, qi, ki, wi, rx, rg, gates = split_cols(h @ w_in[l])
        q = rope(q.reshape(B, S, N_HEADS, HEAD_DIM), positions, ROPE_DIM)
        k = rope(k.reshape(B, S, N_KV_HEADS, HEAD_DIM), positions, ROPE_DIM)
        v = v.reshape(B, S, N_KV_HEADS, HEAD_DIM)
        qi = rope(qi.reshape(B, S, N_IDX_HEADS, IDX_DIM), positions, IDX_ROPE_DIM)
        ki = rope(ki.reshape(B, S, 1, IDX_DIM), positions, IDX_ROPE_DIM)[:, :, 0, :]
        wi = wi * (N_IDX_HEADS ** -0.5 * IDX_DIM ** -0.5)
        y_attn = dsa_attention(q, k, v, qi, ki, wi)
        y_rnn = rglru_branch(rx, rg, conv_w[l], conv_b[l], lru_wa[l], lru_ba[l],
                             lru_wx[l], lru_bx[l], lru_lambda[l])
        g_attn, g_rnn = jnp.split(jax.nn.sigmoid(gates), 2, axis=-1)
        merged = g_attn * (y_attn @ w_attn_branch[l]) + g_rnn * (y_rnn @ w_rnn_branch[l])
        h = layer_norm(DN_ALPHA * h + merged @ w_out[l], ln2_g[l], ln2_b[l])

        h = layer_norm(DN_ALPHA * h + 0.5 * swiglu(h, ffn2_w_in[l], ffn2_w_out[l]), ln3_g[l], ln3_b[l])
    return h
```

```python
import functools

import numpy as np
import jax
import jax.numpy as jnp
from jax import lax
from jax.experimental import pallas as pl
from jax.experimental.pallas import tpu as pltpu

F32 = jnp.float32
BF16 = jnp.bfloat16

D_MODEL = 2048
HEAD_DIM = 128
N_HEADS = D_MODEL // HEAD_DIM
N_KV_HEADS = 4
GROUP = N_HEADS // N_KV_HEADS
ROPE_DIM = HEAD_DIM // 4
ROPE_THETA = 500000.0
N_IDX_HEADS = 16
IDX_DIM = 64
IDX_ROPE_DIM = IDX_DIM // 4
TOPK_MAX = 256
D_RNN = (4 * D_MODEL // 3) // 256 * 256
LRU_BLOCKS = 16
LRU_BLOCK = D_RNN // LRU_BLOCKS
CONV_WIDTH = 4
LRU_C = 8.0
D_FF = (8 * D_MODEL // 3 + 255) // 256 * 256
LN_EPS = 1e-5
DEPTH = 1
DN_ALPHA = (2.0 * DEPTH) ** 0.25

LANES = 128
SUBLANES = 8
VMEM_LIMIT_BYTES = 56 * 1024 * 1024

LRU_GROUPS = 4
LRU_GROUP_W = D_RNN // LRU_GROUPS

INT_MIN = -(2 ** 31)


def _layer_norm(y, g, b):
    mu = jnp.mean(y, axis=-1, keepdims=True)
    d = y - mu
    var = jnp.mean(d * d, axis=-1, keepdims=True)
    return d * lax.rsqrt(var + LN_EPS) * g + b


def _dot(a, b):
    return jnp.dot(a, b, preferred_element_type=F32)


def _dot_nt(a, b):
    return lax.dot_general(a, b, (((1,), (1,)), ((), ())), preferred_element_type=F32)


def _ffn_ln_kernel(x_ref, wa_ref, wb_ref, wo_ref, g_ref, b_ref, o_ref, xb_ref):
    j = pl.program_id(1)

    @pl.when(j == 0)
    def _():
        xb_ref[...] = x_ref[...].astype(BF16)
        o_ref[...] = jnp.zeros_like(o_ref)

    xb = xb_ref[...]
    a = _dot(xb, wa_ref[...])
    b = _dot(xb, wb_ref[...])
    act = (jax.nn.silu(a) * b).astype(BF16)
    o_ref[...] += _dot(act, wo_ref[...])

    @pl.when(j == pl.num_programs(1) - 1)
    def _():
        y = DN_ALPHA * x_ref[...] + 0.5 * o_ref[...]
        o_ref[...] = _layer_norm(y, g_ref[...], b_ref[...])


def _ffn_ln(x, w_in, w_out, g, b, *, tm=512, tf=512):
    m, d = x.shape
    f = w_out.shape[0]
    nf = f // tf
    return pl.pallas_call(
        _ffn_ln_kernel,
        grid=(m // tm, nf),
        in_specs=[
            pl.BlockSpec((tm, d), lambda i, j: (i, 0)),
            pl.BlockSpec((d, tf), lambda i, j: (0, j)),
            pl.BlockSpec((d, tf), lambda i, j: (0, j + nf)),
            pl.BlockSpec((tf, d), lambda i, j: (j, 0)),
            pl.BlockSpec((1, d), lambda i, j: (0, 0)),
            pl.BlockSpec((1, d), lambda i, j: (0, 0)),
        ],
        out_specs=pl.BlockSpec((tm, d), lambda i, j: (i, 0)),
        out_shape=jax.ShapeDtypeStruct((m, d), F32),
        scratch_shapes=[pltpu.VMEM((tm, d), BF16)],
        compiler_params=pltpu.CompilerParams(
            dimension_semantics=("parallel", "arbitrary"),
            vmem_limit_bytes=VMEM_LIMIT_BYTES),
        name="ffn_ln",
    )(x, w_in, w_in, w_out, g, b)


def _rope_tab_kernel(pos_ref, f_ref, o32_ref, o16_ref):
    pos = pos_ref[...].astype(F32)
    for row, o_ref in ((0, o32_ref), (3, o16_ref)):
        ang = pos * f_ref[row:row + 1, :]
        s = jnp.sin(ang)
        o_ref[0] = jnp.cos(ang)
        o_ref[1] = s * f_ref[row + 1:row + 2, :]
        o_ref[2] = s * f_ref[row + 2:row + 3, :]


def _rope_lane_table():
    lane = np.arange(LANES)

    def rows(rot_dim, period):
        half = rot_dim // 2
        inv_freq = jnp.power(ROPE_THETA, -(jnp.arange(half, dtype=F32) * 2.0 / rot_dim))
        l = lane % period
        freq = jnp.where(l < rot_dim, inv_freq[l % half], 0.0)
        plus = ((l >= half) & (l < rot_dim)).astype(np.float32)
        minus = -(l < half).astype(np.float32)
        return [freq, jnp.asarray(plus), jnp.asarray(minus)]

    z = jnp.zeros((LANES,), F32)
    return jnp.stack(rows(ROPE_DIM, HEAD_DIM) + rows(IDX_ROPE_DIM, IDX_DIM) + [z, z]).astype(F32)


def _rope_tables(pos_col, *, tm=1024):
    m = pos_col.shape[0]
    shp = jax.ShapeDtypeStruct((3, m, LANES), F32)
    return pl.pallas_call(
        _rope_tab_kernel,
        grid=(m // tm,),
        in_specs=[pl.BlockSpec((tm, 1), lambda i: (i, 0)),
                  pl.BlockSpec((8, LANES), lambda i: (0, 0))],
        out_specs=[pl.BlockSpec((3, tm, LANES), lambda i: (0, i, 0)),
                   pl.BlockSpec((3, tm, LANES), lambda i: (0, i, 0))],
        out_shape=[shp, shp],
        compiler_params=pltpu.CompilerParams(dimension_semantics=("parallel",)),
        name="rope_tables",
    )(pos_col, _rope_lane_table())


def _proj_kernel(x_ref, w_ref, *rest, tn, rope_shift, n_rope_tiles, n_scale_tiles, scale):
    if rope_shift is None:
        (o_ref,) = rest
        o_ref[...] = _dot(x_ref[...], w_ref[...]).astype(o_ref.dtype)
        return
    tab_ref, o_ref = rest
    j = pl.program_id(1)
    acc = _dot(x_ref[...], w_ref[...])

    @pl.when(j < n_rope_tiles)
    def _():
        c, s_hi, s_lo = tab_ref[0], tab_ref[1], tab_ref[2]
        sc = jnp.where(j < n_scale_tiles, scale, 1.0).astype(F32)
        for t in range(tn // LANES):
            a = acc[:, t * LANES:(t + 1) * LANES]
            r = (a * c + pltpu.roll(a, rope_shift, 1) * s_hi
                 + pltpu.roll(a, LANES - rope_shift, 1) * s_lo)
            o_ref[:, t * LANES:(t + 1) * LANES] = (r * sc).astype(o_ref.dtype)

    @pl.when(j >= n_rope_tiles)
    def _():
        o_ref[...] = acc.astype(o_ref.dtype)


def _proj(xb, w, out_dtype, *, tm, tn, tab=None, rope_shift=None, n_rope_tiles=0,
          n_scale_tiles=0, scale=1.0, name="proj"):
    m, d = xb.shape
    n = w.shape[1]
    in_specs = [pl.BlockSpec((tm, d), lambda i, j: (i, 0)),
                pl.BlockSpec((d, tn), lambda i, j: (0, j))]
    args = [xb, w]
    if rope_shift is not None:
        in_specs.append(pl.BlockSpec((3, tm, LANES), lambda i, j: (0, i, 0)))
        args.append(tab)
    return pl.pallas_call(
        functools.partial(_proj_kernel, tn=tn, rope_shift=rope_shift, n_rope_tiles=n_rope_tiles,
                          n_scale_tiles=n_scale_tiles, scale=scale),
        grid=(m // tm, n // tn),
        in_specs=in_specs,
        out_specs=pl.BlockSpec((tm, tn), lambda i, j: (i, j)),
        out_shape=jax.ShapeDtypeStruct((m, n), out_dtype),
        compiler_params=pltpu.CompilerParams(
            dimension_semantics=("parallel", "arbitrary"),
            vmem_limit_bytes=VMEM_LIMIT_BYTES),
        name=name,
    )(*args)


def _rglru_kernel(rx_ref, rg_ref, cw_ref, cb_ref, wa_ref, wx_ref, ba_ref, bx_ref, lam_ref,
                  o_ref, xbuf, a_s, b_s, h_s, *, tt):
    t = pl.program_id(2)
    w = xbuf.shape[1]

    @pl.when(t == 0)
    def _():
        xbuf[0:SUBLANES, :] = jnp.zeros((SUBLANES, w), F32)
        h_s[...] = jnp.zeros_like(h_s)

    xbuf[SUBLANES:SUBLANES + tt, :] = rx_ref[...]
    xc = cb_ref[...]
    for k in range(CONV_WIDTH):
        off = SUBLANES - (CONV_WIDTH - 1) + k
        xc = xc + cw_ref[k:k + 1, :] * xbuf[off:off + tt, :]
    xbuf[0:SUBLANES, :] = rx_ref[tt - SUBLANES:tt, :]

    xcb = xc.astype(BF16)
    r = jax.nn.sigmoid(_dot(xcb, wa_ref[0]) + ba_ref[...])
    ig = jax.nn.sigmoid(_dot(xcb, wx_ref[0]) + bx_ref[...])
    log_a = -LRU_C * r * jax.nn.softplus(-lam_ref[...])
    a = jnp.exp(log_a)
    a_s[...] = a
    b_s[...] = jnp.sqrt(-jnp.tanh(log_a) * (a * a + 1.0)) * (ig * xc)

    row = lax.broadcasted_iota(jnp.int32, (SUBLANES, w), 0)

    def body(g, h):
        sl = pl.ds(pl.multiple_of(g * SUBLANES, SUBLANES), SUBLANES)
        a8 = a_s[sl, :]
        b8 = b_s[sl, :]
        for d in (1, 2, 4):
            keep = row >= d
            a_sh = jnp.where(keep, pltpu.roll(a8, d, 0), 1.0)
            b_sh = jnp.where(keep, pltpu.roll(b8, d, 0), 0.0)
            b8 = a8 * b_sh + b8
            a8 = a8 * a_sh
        hrows = a8 * h + b8
        b_s[sl, :] = hrows
        return jnp.broadcast_to(hrows[SUBLANES - 1:SUBLANES, :], (SUBLANES, w))

    h_s[...] = lax.fori_loop(0, tt // SUBLANES, body, h_s[...])
    o_ref[...] = (b_s[...] * jax.nn.gelu(rg_ref[...])).astype(o_ref.dtype)


def _rglru(rxg, conv_w, conv_b, wa_bd, wx_bd, ba, bx, lam, *, batch, seq, tt=512):
    m = rxg.shape[0]
    nt = seq // tt
    gw = LRU_GROUP_W
    vec = lambda: pl.BlockSpec((1, gw), lambda b, g, t: (0, g))
    return pl.pallas_call(
        functools.partial(_rglru_kernel, tt=tt),
        grid=(batch, LRU_GROUPS, nt),
        in_specs=[
            pl.BlockSpec((tt, gw), lambda b, g, t: (b * nt + t, g)),
            pl.BlockSpec((tt, gw), lambda b, g, t: (b * nt + t, g + LRU_GROUPS)),
            pl.BlockSpec((CONV_WIDTH, gw), lambda b, g, t: (0, g)),
            vec(),
            pl.BlockSpec((1, gw, gw), lambda b, g, t: (g, 0, 0)),
            pl.BlockSpec((1, gw, gw), lambda b, g, t: (g, 0, 0)),
            vec(), vec(), vec(),
        ],
        out_specs=pl.BlockSpec((tt, gw), lambda b, g, t: (b * nt + t, g)),
        out_shape=jax.ShapeDtypeStruct((m, D_RNN), BF16),
        scratch_shapes=[pltpu.VMEM((tt + SUBLANES, gw), F32), pltpu.VMEM((tt, gw), F32),
                        pltpu.VMEM((tt, gw), F32), pltpu.VMEM((SUBLANES, gw), F32)],
        compiler_params=pltpu.CompilerParams(
            dimension_semantics=("parallel", "parallel", "arbitrary"),
            vmem_limit_bytes=VMEM_LIMIT_BYTES),
        name="rglru",
    )(rxg, rxg, conv_w, conv_b, wa_bd, wx_bd, ba, bx, lam)


def _dsa_kernel(qi_ref, kie_ref, kio_ref, wi_ref, q_ref, k_ref, v_ref, o_ref,
                score_ref, bias_ref, *, tq, sc, topk):
    i = pl.program_id(1)
    s_len = k_ref.shape[0]
    n_pair = N_IDX_HEADS // 2

    qi = qi_ref[...]
    qs = jnp.concatenate([qi[:, j * LANES:(j + 1) * LANES] for j in range(n_pair)], axis=0)
    w = wi_ref[...] * (N_IDX_HEADS ** -0.5 * IDX_DIM ** -0.5)
    for c in range(s_len // sc):
        le = _dot_nt(qs, kie_ref[c * sc:(c + 1) * sc, :])
        lo = _dot_nt(qs, kio_ref[c * sc:(c + 1) * sc, :])
        acc = jnp.zeros((tq, sc), F32)
        for j in range(n_pair):
            acc = acc + jnp.maximum(le[j * tq:(j + 1) * tq], 0.0) * w[:, 2 * j:2 * j + 1]
            acc = acc + jnp.maximum(lo[j * tq:(j + 1) * tq], 0.0) * w[:, 2 * j + 1:2 * j + 2]
        score_ref[:, c * sc:(c + 1) * sc] = acc

    qpos = i * tq + lax.broadcasted_iota(jnp.int32, (tq, s_len), 0)
    kpos = lax.broadcasted_iota(jnp.int32, (tq, s_len), 1)
    score = jnp.where(kpos <= qpos, score_ref[...], -jnp.inf)

    def key_to_float(key):
        return pltpu.bitcast(jnp.where(key < 0, key ^ 0x7FFFFFFF, key), F32)

    def search(it, key):
        cand = key ^ lax.shift_left(jnp.int32(1), 31 - it)
        cnt = jnp.sum(jnp.where(score >= key_to_float(cand), 1.0, 0.0), axis=1, keepdims=True)
        return jnp.where(cnt >= topk, cand, key)

    key = lax.fori_loop(0, 32, search, jnp.full((tq, 1), INT_MIN, jnp.int32))
    thr = key_to_float(jnp.maximum(key, INT_MIN + 0x00800000))
    ge = score >= thr
    cnt = jnp.sum(jnp.where(ge, 1.0, 0.0), axis=1, keepdims=True)
    bias_ref[...] = jnp.where(ge, 0.0, -jnp.inf).astype(F32)

    @pl.when(jnp.max(cnt) > topk)
    def _():
        tw = 2 * LANES
        need = topk - jnp.sum(jnp.where(score > thr, 1.0, 0.0), axis=1, keepdims=True)
        tri = jnp.where(lax.broadcasted_iota(jnp.int32, (tw, tw), 0)
                        <= lax.broadcasted_iota(jnp.int32, (tw, tw), 1), 1.0, 0.0).astype(BF16)
        carry = jnp.zeros((tq, 1), F32)
        for c in range(s_len // tw):
            kc = score[:, c * tw:(c + 1) * tw]
            eqc = jnp.where(kc == thr, 1.0, 0.0)
            rank = _dot(eqc.astype(BF16), tri) + carry
            take = (kc > thr) | ((kc == thr) & (rank <= need))
            bias_ref[:, c * tw:(c + 1) * tw] = jnp.where(take, 0.0, -jnp.inf).astype(F32)
            carry = carry + jnp.sum(eqc, axis=1, keepdims=True)

    bias = bias_ref[...]
    q = q_ref[...]
    for g in range(N_KV_HEADS):
        qg = jnp.concatenate(
            [q[:, (g * GROUP + h) * HEAD_DIM:(g * GROUP + h + 1) * HEAD_DIM] for h in range(GROUP)],
            axis=0)
        s = _dot_nt(qg, k_ref[:, g * HEAD_DIM:(g + 1) * HEAD_DIM])
        s = s.reshape(GROUP, tq, s_len) + bias[None]
        p = jnp.exp(s - jnp.max(s, axis=-1, keepdims=True))
        l = jnp.sum(p, axis=-1, keepdims=True)
        o = _dot(p.astype(BF16).reshape(GROUP * tq, s_len), v_ref[:, g * HEAD_DIM:(g + 1) * HEAD_DIM])
        o = o.reshape(GROUP, tq, HEAD_DIM) / l
        for h in range(GROUP):
            col = (g * GROUP + h) * HEAD_DIM
            o_ref[:, col:col + HEAD_DIM] = o[h].astype(o_ref.dtype)


def _dsa(qkv, qik, gw, *, batch, seq, tq=128, sc=512):
    m = qkv.shape[0]
    nq = seq // tq
    topk = min(TOPK_MAX, seq // 4)
    kv_w = N_KV_HEADS * HEAD_DIM
    qi_w = N_IDX_HEADS * IDX_DIM
    return pl.pallas_call(
        functools.partial(_dsa_kernel, tq=tq, sc=sc, topk=topk),
        grid=(batch, nq),
        in_specs=[
            pl.BlockSpec((tq, qi_w), lambda b, i: (b * nq + i, 0)),
            pl.BlockSpec((seq, LANES), lambda b, i: (b, qi_w // LANES)),
            pl.BlockSpec((seq, LANES), lambda b, i: (b, qi_w // LANES + 1)),
            pl.BlockSpec((tq, LANES), lambda b, i: (b * nq + i, 2 * D_MODEL // LANES)),
            pl.BlockSpec((tq, D_MODEL), lambda b, i: (b * nq + i, 0)),
            pl.BlockSpec((seq, kv_w), lambda b, i: (b, D_MODEL // kv_w)),
            pl.BlockSpec((seq, kv_w), lambda b, i: (b, D_MODEL // kv_w + 1)),
        ],
        out_specs=pl.BlockSpec((tq, D_MODEL), lambda b, i: (b * nq + i, 0)),
        out_shape=jax.ShapeDtypeStruct((m, D_MODEL), BF16),
        scratch_shapes=[pltpu.VMEM((tq, seq), F32), pltpu.VMEM((tq, seq), F32)],
        compiler_params=pltpu.CompilerParams(
            dimension_semantics=("parallel", "arbitrary"),
            vmem_limit_bytes=VMEM_LIMIT_BYTES),
        name="dsa",
    )(qik, qik, qik, gw, qkv, qkv, qkv)


def _merge_ln_kernel(ya_ref, yr_ref, ga_ref, gr_ref, wa_ref, wr_ref, wo_ref, h_ref, g_ref, b_ref, o_ref):
    j = pl.program_id(1)

    @pl.when(j == 0)
    def _():
        o_ref[...] = jnp.zeros_like(o_ref)

    merged = (jax.nn.sigmoid(ga_ref[...]) * _dot(ya_ref[...], wa_ref[...])
              + jax.nn.sigmoid(gr_ref[...]) * _dot(yr_ref[...], wr_ref[...]))
    o_ref[...] += _dot(merged.astype(BF16), wo_ref[...])

    @pl.when(j == pl.num_programs(1) - 1)
    def _():
        y = DN_ALPHA * h_ref[...] + o_ref[...]
        o_ref[...] = _layer_norm(y, g_ref[...], b_ref[...])


def _merge_ln(y_attn, y_rnn, gw, w_attn, w_rnn, w_out, h, g, b, *, tm=512, tn=512):
    m, d = h.shape
    nn = d // tn
    return pl.pallas_call(
        _merge_ln_kernel,
        grid=(m // tm, nn),
        in_specs=[
            pl.BlockSpec((tm, y_attn.shape[1]), lambda i, j: (i, 0)),
            pl.BlockSpec((tm, y_rnn.shape[1]), lambda i, j: (i, 0)),
            pl.BlockSpec((tm, tn), lambda i, j: (i, j)),
            pl.BlockSpec((tm, tn), lambda i, j: (i, j + nn)),
            pl.BlockSpec((w_attn.shape[0], tn), lambda i, j: (0, j)),
            pl.BlockSpec((w_rnn.shape[0], tn), lambda i, j: (0, j)),
            pl.BlockSpec((tn, d), lambda i, j: (j, 0)),
            pl.BlockSpec((tm, d), lambda i, j: (i, 0)),
            pl.BlockSpec((1, d), lambda i, j: (0, 0)),
            pl.BlockSpec((1, d), lambda i, j: (0, 0)),
        ],
        out_specs=pl.BlockSpec((tm, d), lambda i, j: (i, 0)),
        out_shape=jax.ShapeDtypeStruct((m, d), F32),
        compiler_params=pltpu.CompilerParams(
            dimension_semantics=("parallel", "arbitrary"),
            vmem_limit_bytes=VMEM_LIMIT_BYTES),
        name="merge_ln",
    )(y_attn, y_rnn, gw, gw, w_attn, w_rnn, w_out, h, g, b)


def _block_diag_groups(w):
    per = LRU_BLOCKS // LRU_GROUPS
    w4 = w.reshape(LRU_GROUPS, per, LRU_BLOCK, LRU_BLOCK)
    bd = jnp.einsum("gaij,ab->gaibj", w4, jnp.eye(per, dtype=w.dtype))
    return bd.reshape(LRU_GROUPS, LRU_GROUP_W, LRU_GROUP_W)


def kernel(x, positions, ffn1_w_in, ffn1_w_out, ln1_g, ln1_b, w_in, conv_w, conv_b, lru_wa, lru_ba,
           lru_wx, lru_bx, lru_lambda, w_attn_branch, w_rnn_branch, w_out, ln2_g, ln2_b, ffn2_w_in,
           ffn2_w_out, ln3_g, ln3_b):
    batch, seq, d = x.shape
    m = batch * seq
    h = x.reshape(m, d)
    pos_col = positions.reshape(m, 1)
    tab32, tab16 = _rope_tables(pos_col)

    for l in range(DEPTH):
        h = _ffn_ln(h, ffn1_w_in[l].astype(BF16), ffn1_w_out[l].astype(BF16), ln1_g[l][None], ln1_b[l][None])
        hb = h.astype(BF16)

        wl = w_in[l]
        o_q, o_k, o_v = 0, D_MODEL, D_MODEL + N_KV_HEADS * HEAD_DIM
        o_qi = o_v + N_KV_HEADS * HEAD_DIM
        o_ki = o_qi + N_IDX_HEADS * IDX_DIM
        o_wi = o_ki + IDX_DIM
        o_rx = o_wi + N_IDX_HEADS
        o_gt = o_rx + 2 * D_RNN
        zk = jnp.zeros((d, IDX_DIM), wl.dtype)
        w_ki = wl[:, o_ki:o_wi]
        w_qkv = wl[:, o_q:o_qi].astype(BF16)
        w_qik = jnp.concatenate([wl[:, o_qi:o_ki], w_ki, zk, zk, w_ki], axis=1).astype(BF16)
        w_rxg = wl[:, o_rx:o_gt].astype(BF16)
        pad = jnp.zeros((d, 512 - N_IDX_HEADS), wl.dtype)
        w_gw = jnp.concatenate([wl[:, o_gt:], wl[:, o_wi:o_rx], pad], axis=1).astype(BF16)

        qkv = _proj(hb, w_qkv, BF16, tm=1024, tn=512, tab=tab32, rope_shift=ROPE_DIM // 2,
                    n_rope_tiles=(D_MODEL + N_KV_HEADS * HEAD_DIM) // 512,
                    n_scale_tiles=D_MODEL // 512, scale=HEAD_DIM ** -0.5, name="proj_qkv")
        qik = _proj(hb, w_qik, BF16, tm=1024, tn=256, tab=tab16, rope_shift=IDX_ROPE_DIM // 2,
                    n_rope_tiles=w_qik.shape[1] // 256, name="proj_idx")
        rxg = _proj(hb, w_rxg, F32, tm=1024, tn=512, name="proj_rnn")
        gw = _proj(hb, w_gw, F32, tm=1024, tn=512, name="proj_gate")

        y_rnn = _rglru(rxg, conv_w[l], conv_b[l][None],
                       _block_diag_groups(lru_wa[l]).astype(BF16), _block_diag_groups(lru_wx[l]).astype(BF16),
                       lru_ba[l][None], lru_bx[l][None], lru_lambda[l][None], batch=batch, seq=seq)
        y_attn = _dsa(qkv, qik, gw, batch=batch, seq=seq)

        h = _merge_ln(y_attn, y_rnn, gw, w_attn_branch[l].astype(BF16), w_rnn_branch[l].astype(BF16),
                      w_out[l].astype(BF16), h, ln2_g[l][None], ln2_b[l][None])
        h = _ffn_ln(h, ffn2_w_in[l].astype(BF16), ffn2_w_out[l].astype(BF16), ln3_g[l][None], ln3_b[l][None])
    return h.reshape(batch, seq, d)
```

```python
import functools

import numpy as np
import jax
import jax.numpy as jnp
from jax import lax
from jax.experimental import pallas as pl
from jax.experimental.pallas import tpu as pltpu

F32 = jnp.float32
BF16 = jnp.bfloat16

D_MODEL = 2048
HEAD_DIM = 128
N_HEADS = D_MODEL // HEAD_DIM
N_KV_HEADS = 4
GROUP = N_HEADS // N_KV_HEADS
ROPE_DIM = HEAD_DIM // 4
ROPE_THETA = 500000.0
N_IDX_HEADS = 16
IDX_DIM = 64
IDX_ROPE_DIM = IDX_DIM // 4
TOPK_MAX = 256
D_RNN = (4 * D_MODEL // 3) // 256 * 256
LRU_BLOCKS = 16
LRU_BLOCK = D_RNN // LRU_BLOCKS
CONV_WIDTH = 4
LRU_C = 8.0
D_FF = (8 * D_MODEL // 3 + 255) // 256 * 256
LN_EPS = 1e-5
DEPTH = 1
DN_ALPHA = (2.0 * DEPTH) ** 0.25

LANES = 128
SUBLANES = 8
VMEM_LIMIT_BYTES = 56 * 1024 * 1024

LRU_GROUPS = 4
LRU_GROUP_W = D_RNN // LRU_GROUPS

INT_MIN = -(2 ** 31)
LOG2_E = 1.4426950408889634


def _layer_norm(y, g, b):
    mu = jnp.mean(y, axis=-1, keepdims=True)
    d = y - mu
    var = jnp.mean(d * d, axis=-1, keepdims=True)
    return d * lax.rsqrt(var + LN_EPS) * g + b


def _dot(a, b):
    return jnp.dot(a, b, preferred_element_type=F32)


def _dot_nt(a, b):
    return lax.dot_general(a, b, (((1,), (1,)), ((), ())), preferred_element_type=F32)


def _ffn_ln_kernel(x_ref, wa_ref, wb_ref, wo_ref, g_ref, b_ref, o_ref, xb_ref):
    j = pl.program_id(1)

    @pl.when(j == 0)
    def _():
        xb_ref[...] = x_ref[...].astype(BF16)
        o_ref[...] = jnp.zeros_like(o_ref)

    xb = xb_ref[...]
    a = _dot(xb, wa_ref[...])
    b = _dot(xb, wb_ref[...])
    act = (jax.nn.silu(a) * b).astype(BF16)
    o_ref[...] += _dot(act, wo_ref[...])

    @pl.when(j == pl.num_programs(1) - 1)
    def _():
        y = DN_ALPHA * x_ref[...] + 0.5 * o_ref[...]
        o_ref[...] = _layer_norm(y, g_ref[...], b_ref[...])


def _ffn_ln(x, w_in, w_out, g, b, *, tm=512, tf=512):
    m, d = x.shape
    f = w_out.shape[0]
    nf = f // tf
    return pl.pallas_call(
        _ffn_ln_kernel,
        grid=(m // tm, nf),
        in_specs=[
            pl.BlockSpec((tm, d), lambda i, j: (i, 0)),
            pl.BlockSpec((d, tf), lambda i, j: (0, j)),
            pl.BlockSpec((d, tf), lambda i, j: (0, j + nf)),
            pl.BlockSpec((tf, d), lambda i, j: (j, 0)),
            pl.BlockSpec((1, d), lambda i, j: (0, 0)),
            pl.BlockSpec((1, d), lambda i, j: (0, 0)),
        ],
        out_specs=pl.BlockSpec((tm, d), lambda i, j: (i, 0)),
        out_shape=jax.ShapeDtypeStruct((m, d), F32),
        scratch_shapes=[pltpu.VMEM((tm, d), BF16)],
        compiler_params=pltpu.CompilerParams(
            dimension_semantics=("parallel", "arbitrary"),
            vmem_limit_bytes=VMEM_LIMIT_BYTES),
        name="ffn_ln",
    )(x, w_in, w_in, w_out, g, b)


def _rope_tab_kernel(pos_ref, f_ref, o32_ref, o16_ref):
    pos = pos_ref[...].astype(F32)
    for row, o_ref in ((0, o32_ref), (3, o16_ref)):
        ang = pos * f_ref[row:row + 1, :]
        s = jnp.sin(ang)
        o_ref[0] = jnp.cos(ang)
        o_ref[1] = s * f_ref[row + 1:row + 2, :]
        o_ref[2] = s * f_ref[row + 2:row + 3, :]


def _rope_lane_table():
    lane = np.arange(LANES)

    def rows(rot_dim, period):
        half = rot_dim // 2
        inv_freq = jnp.power(ROPE_THETA, -(jnp.arange(half, dtype=F32) * 2.0 / rot_dim))
        l = lane % period
        freq = jnp.where(l < rot_dim, inv_freq[l % half], 0.0)
        plus = ((l >= half) & (l < rot_dim)).astype(np.float32)
        minus = -(l < half).astype(np.float32)
        return [freq, jnp.asarray(plus), jnp.asarray(minus)]

    z = jnp.zeros((LANES,), F32)
    return jnp.stack(rows(ROPE_DIM, HEAD_DIM) + rows(IDX_ROPE_DIM, IDX_DIM) + [z, z]).astype(F32)


def _rope_tables(pos_col, *, tm=1024):
    m = pos_col.shape[0]
    shp = jax.ShapeDtypeStruct((3, m, LANES), F32)
    return pl.pallas_call(
        _rope_tab_kernel,
        grid=(m // tm,),
        in_specs=[pl.BlockSpec((tm, 1), lambda i: (i, 0)),
                  pl.BlockSpec((8, LANES), lambda i: (0, 0))],
        out_specs=[pl.BlockSpec((3, tm, LANES), lambda i: (0, i, 0)),
                   pl.BlockSpec((3, tm, LANES), lambda i: (0, i, 0))],
        out_shape=[shp, shp],
        compiler_params=pltpu.CompilerParams(dimension_semantics=("parallel",)),
        name="rope_tables",
    )(pos_col, _rope_lane_table())


def _proj_kernel(x_ref, w_ref, *rest, tn, rope_shift, n_rope_tiles, n_scale_tiles, scale):
    if rope_shift is None:
        (o_ref,) = rest
        o_ref[...] = _dot(x_ref[...], w_ref[...]).astype(o_ref.dtype)
        return
    tab_ref, o_ref = rest
    j = pl.program_id(1)
    acc = _dot(x_ref[...], w_ref[...])

    @pl.when(j < n_rope_tiles)
    def _():
        c, s_hi, s_lo = tab_ref[0], tab_ref[1], tab_ref[2]
        sc = jnp.where(j < n_scale_tiles, scale, 1.0).astype(F32)
        for t in range(tn // LANES):
            a = acc[:, t * LANES:(t + 1) * LANES]
            r = (a * c + pltpu.roll(a, rope_shift, 1) * s_hi
                 + pltpu.roll(a, LANES - rope_shift, 1) * s_lo)
            o_ref[:, t * LANES:(t + 1) * LANES] = (r * sc).astype(o_ref.dtype)

    @pl.when(j >= n_rope_tiles)
    def _():
        o_ref[...] = acc.astype(o_ref.dtype)


def _proj(xb, w, out_dtype, *, tm, tn, tab=None, rope_shift=None, n_rope_tiles=0,
          n_scale_tiles=0, scale=1.0, name="proj"):
    m, d = xb.shape
    n = w.shape[1]
    in_specs = [pl.BlockSpec((tm, d), lambda i, j: (i, 0)),
                pl.BlockSpec((d, tn), lambda i, j: (0, j))]
    args = [xb, w]
    if rope_shift is not None:
        in_specs.append(pl.BlockSpec((3, tm, LANES), lambda i, j: (0, i, 0)))
        args.append(tab)
    return pl.pallas_call(
        functools.partial(_proj_kernel, tn=tn, rope_shift=rope_shift, n_rope_tiles=n_rope_tiles,
                          n_scale_tiles=n_scale_tiles, scale=scale),
        grid=(m // tm, n // tn),
        in_specs=in_specs,
        out_specs=pl.BlockSpec((tm, tn), lambda i, j: (i, j)),
        out_shape=jax.ShapeDtypeStruct((m, n), out_dtype),
        compiler_params=pltpu.CompilerParams(
            dimension_semantics=("parallel", "arbitrary"),
            vmem_limit_bytes=VMEM_LIMIT_BYTES),
        name=name,
    )(*args)


def _rglru_kernel(rx_ref, rg_ref, cw_ref, cb_ref, wa_ref, wx_ref, ba_ref, bx_ref, lam_ref,
                  o_ref, xbuf, a_s, b_s, h_s, *, tt):
    t = pl.program_id(2)
    w = xbuf.shape[1]

    @pl.when(t == 0)
    def _():
        xbuf[0:SUBLANES, :] = jnp.zeros((SUBLANES, w), F32)
        h_s[...] = jnp.zeros_like(h_s)

    xbuf[SUBLANES:SUBLANES + tt, :] = rx_ref[...]
    xc = cb_ref[...]
    for k in range(CONV_WIDTH):
        off = SUBLANES - (CONV_WIDTH - 1) + k
        xc = xc + cw_ref[k:k + 1, :] * xbuf[off:off + tt, :]
    xbuf[0:SUBLANES, :] = rx_ref[tt - SUBLANES:tt, :]

    xcb = xc.astype(BF16)
    r = jax.nn.sigmoid(_dot(xcb, wa_ref[0]) + ba_ref[...])
    ig = jax.nn.sigmoid(_dot(xcb, wx_ref[0]) + bx_ref[...])
    log_a = -LRU_C * r * jax.nn.softplus(-lam_ref[...])
    a = jnp.exp(log_a)
    a_s[...] = a
    b_s[...] = jnp.sqrt(-jnp.tanh(log_a) * (a * a + 1.0)) * (ig * xc)

    row = lax.broadcasted_iota(jnp.int32, (SUBLANES, w), 0)

    def body(g, h):
        sl = pl.ds(pl.multiple_of(g * SUBLANES, SUBLANES), SUBLANES)
        a8 = a_s[sl, :]
        b8 = b_s[sl, :]
        for d in (1, 2, 4):
            keep = row >= d
            a_sh = jnp.where(keep, pltpu.roll(a8, d, 0), 1.0)
            b_sh = jnp.where(keep, pltpu.roll(b8, d, 0), 0.0)
            b8 = a8 * b_sh + b8
            a8 = a8 * a_sh
        hrows = a8 * h + b8
        b_s[sl, :] = hrows
        return jnp.broadcast_to(hrows[SUBLANES - 1:SUBLANES, :], (SUBLANES, w))

    h_s[...] = lax.fori_loop(0, tt // SUBLANES, body, h_s[...])
    o_ref[...] = (b_s[...] * jax.nn.gelu(rg_ref[...])).astype(o_ref.dtype)


def _rglru(rxg, conv_w, conv_b, wa_bd, wx_bd, ba, bx, lam, *, batch, seq, tt=512):
    m = rxg.shape[0]
    nt = seq // tt
    gw = LRU_GROUP_W
    vec = lambda: pl.BlockSpec((1, gw), lambda b, g, t: (0, g))
    return pl.pallas_call(
        functools.partial(_rglru_kernel, tt=tt),
        grid=(batch, LRU_GROUPS, nt),
        in_specs=[
            pl.BlockSpec((tt, gw), lambda b, g, t: (b * nt + t, g)),
            pl.BlockSpec((tt, gw), lambda b, g, t: (b * nt + t, g + LRU_GROUPS)),
            pl.BlockSpec((CONV_WIDTH, gw), lambda b, g, t: (0, g)),
            vec(),
            pl.BlockSpec((1, gw, gw), lambda b, g, t: (g, 0, 0)),
            pl.BlockSpec((1, gw, gw), lambda b, g, t: (g, 0, 0)),
            vec(), vec(), vec(),
        ],
        out_specs=pl.BlockSpec((tt, gw), lambda b, g, t: (b * nt + t, g)),
        out_shape=jax.ShapeDtypeStruct((m, D_RNN), BF16),
        scratch_shapes=[pltpu.VMEM((tt + SUBLANES, gw), F32), pltpu.VMEM((tt, gw), F32),
                        pltpu.VMEM((tt, gw), F32), pltpu.VMEM((SUBLANES, gw), F32)],
        compiler_params=pltpu.CompilerParams(
            dimension_semantics=("parallel", "parallel", "arbitrary"),
            vmem_limit_bytes=VMEM_LIMIT_BYTES),
        name="rglru",
    )(rxg, rxg, conv_w, conv_b, wa_bd, wx_bd, ba, bx, lam)


def _dsa_kernel(qi_ref, kie_ref, kio_ref, wi_ref, q_ref, k_ref, v_ref, o_ref,
                score_ref, bias_ref, *, tq, ck, topk):
    i = pl.program_id(1)
    s_len = k_ref.shape[0]
    n_chunks = lax.div(i * tq + (tq - 1), ck) + 1
    for n in range(1, s_len // ck + 1):
        @pl.when(n_chunks == n)
        def _(n=n):
            _dsa_body(qi_ref, kie_ref, kio_ref, wi_ref, q_ref, k_ref, v_ref, o_ref, score_ref, bias_ref,
                      i=i, tq=tq, ck=ck, topk=topk, s_eff=n * ck)


def _dsa_body(qi_ref, kie_ref, kio_ref, wi_ref, q_ref, k_ref, v_ref, o_ref, score_ref, bias_ref,
              *, i, tq, ck, topk, s_eff):
    n_pair = N_IDX_HEADS // 2

    qi = qi_ref[...]
    qs = jnp.concatenate([qi[:, j * LANES:(j + 1) * LANES] for j in range(n_pair)], axis=0)
    w = wi_ref[...] * (N_IDX_HEADS ** -0.5 * IDX_DIM ** -0.5)
    for c in range(s_eff // ck):
        le = _dot_nt(qs, kie_ref[c * ck:(c + 1) * ck, :])
        lo = _dot_nt(qs, kio_ref[c * ck:(c + 1) * ck, :])
        acc = jnp.zeros((tq, ck), F32)
        for j in range(n_pair):
            acc = acc + jnp.maximum(le[j * tq:(j + 1) * tq], 0.0) * w[:, 2 * j:2 * j + 1]
            acc = acc + jnp.maximum(lo[j * tq:(j + 1) * tq], 0.0) * w[:, 2 * j + 1:2 * j + 2]
        score_ref[:, c * ck:(c + 1) * ck] = acc

    qpos = i * tq + lax.broadcasted_iota(jnp.int32, (tq, s_eff), 0)
    kpos = lax.broadcasted_iota(jnp.int32, (tq, s_eff), 1)
    score = jnp.where(kpos <= qpos, score_ref[:, :s_eff], -jnp.inf)

    def key_to_float(key):
        return pltpu.bitcast(jnp.where(key < 0, key ^ 0x7FFFFFFF, key), F32)

    def search(it, key):
        cand = key ^ lax.shift_left(jnp.int32(1), 31 - it)
        cnt = jnp.sum(jnp.where(score >= key_to_float(cand), 1.0, 0.0), axis=1, keepdims=True)
        return jnp.where(cnt >= topk, cand, key)

    key = lax.fori_loop(0, 32, search, jnp.full((tq, 1), INT_MIN, jnp.int32))
    thr = key_to_float(jnp.maximum(key, INT_MIN + 0x00800000))
    ge = score >= thr
    cnt = jnp.sum(jnp.where(ge, 1.0, 0.0), axis=1, keepdims=True)
    bias_ref[:, :s_eff] = jnp.where(ge, 0.0, -jnp.inf).astype(F32)

    @pl.when(jnp.max(cnt) > topk)
    def _():
        tw = 2 * LANES
        need = topk - jnp.sum(jnp.where(score > thr, 1.0, 0.0), axis=1, keepdims=True)
        tri = jnp.where(lax.broadcasted_iota(jnp.int32, (tw, tw), 0)
                        <= lax.broadcasted_iota(jnp.int32, (tw, tw), 1), 1.0, 0.0).astype(BF16)
        carry = jnp.zeros((tq, 1), F32)
        for c in range(s_eff // tw):
            kc = score[:, c * tw:(c + 1) * tw]
            eqc = jnp.where(kc == thr, 1.0, 0.0)
            rank = _dot(eqc.astype(BF16), tri) + carry
            take = (kc > thr) | ((kc == thr) & (rank <= need))
            bias_ref[:, c * tw:(c + 1) * tw] = jnp.where(take, 0.0, -jnp.inf).astype(F32)
            carry = carry + jnp.sum(eqc, axis=1, keepdims=True)

    bias = bias_ref[:, :s_eff]
    q = q_ref[...]
    ones = jnp.ones((s_eff, HEAD_DIM), BF16)
    for g in range(N_KV_HEADS):
        qg = jnp.concatenate(
            [q[:, (g * GROUP + h) * HEAD_DIM:(g * GROUP + h + 1) * HEAD_DIM] for h in range(GROUP)],
            axis=0)
        s = _dot_nt(qg, k_ref[:s_eff, g * HEAD_DIM:(g + 1) * HEAD_DIM])
        s = s.reshape(GROUP, tq, s_eff) + bias[None]
        p = jnp.exp2(s - jnp.max(s, axis=-1, keepdims=True)).astype(BF16)
        v_aug = jnp.concatenate([v_ref[:s_eff, g * HEAD_DIM:(g + 1) * HEAD_DIM], ones], axis=1)
        o = _dot(p.reshape(GROUP * tq, s_eff), v_aug)
        o = o[:, :HEAD_DIM] / o[:, HEAD_DIM:]
        for h in range(GROUP):
            col = (g * GROUP + h) * HEAD_DIM
            o_ref[:, col:col + HEAD_DIM] = o[h * tq:(h + 1) * tq].astype(o_ref.dtype)


def _dsa(qkv, qik, gw, *, batch, seq, tq=256, ck=512):
    m = qkv.shape[0]
    nq = seq // tq
    topk = min(TOPK_MAX, seq // 4)
    kv_w = N_KV_HEADS * HEAD_DIM
    qi_w = N_IDX_HEADS * IDX_DIM
    return pl.pallas_call(
        functools.partial(_dsa_kernel, tq=tq, ck=ck, topk=topk),
        grid=(batch, nq),
        in_specs=[
            pl.BlockSpec((tq, qi_w), lambda b, i: (b * nq + i, 0)),
            pl.BlockSpec((seq, LANES), lambda b, i: (b, qi_w // LANES)),
            pl.BlockSpec((seq, LANES), lambda b, i: (b, qi_w // LANES + 1)),
            pl.BlockSpec((tq, LANES), lambda b, i: (b * nq + i, 2 * D_MODEL // LANES)),
            pl.BlockSpec((tq, D_MODEL), lambda b, i: (b * nq + i, 0)),
            pl.BlockSpec((seq, kv_w), lambda b, i: (b, D_MODEL // kv_w)),
            pl.BlockSpec((seq, kv_w), lambda b, i: (b, D_MODEL // kv_w + 1)),
        ],
        out_specs=pl.BlockSpec((tq, D_MODEL), lambda b, i: (b * nq + i, 0)),
        out_shape=jax.ShapeDtypeStruct((m, D_MODEL), BF16),
        scratch_shapes=[pltpu.VMEM((tq, seq), F32), pltpu.VMEM((tq, seq), F32)],
        compiler_params=pltpu.CompilerParams(
            dimension_semantics=("parallel", "arbitrary"),
            vmem_limit_bytes=VMEM_LIMIT_BYTES),
        name="dsa",
    )(qik, qik, qik, gw, qkv, qkv, qkv)


def _merge_ln_kernel(ya_ref, yr_ref, ga_ref, gr_ref, wa_ref, wr_ref, wo_ref, h_ref, g_ref, b_ref, o_ref):
    j = pl.program_id(1)

    @pl.when(j == 0)
    def _():
        o_ref[...] = jnp.zeros_like(o_ref)

    merged = (jax.nn.sigmoid(ga_ref[...]) * _dot(ya_ref[...], wa_ref[...])
              + jax.nn.sigmoid(gr_ref[...]) * _dot(yr_ref[...], wr_ref[...]))
    o_ref[...] += _dot(merged.astype(BF16), wo_ref[...])

    @pl.when(j == pl.num_programs(1) - 1)
    def _():
        y = DN_ALPHA * h_ref[...] + o_ref[...]
        o_ref[...] = _layer_norm(y, g_ref[...], b_ref[...])


def _merge_ln(y_attn, y_rnn, gw, w_attn, w_rnn, w_out, h, g, b, *, tm=512, tn=512):
    m, d = h.shape
    nn = d // tn
    return pl.pallas_call(
        _merge_ln_kernel,
        grid=(m // tm, nn),
        in_specs=[
            pl.BlockSpec((tm, y_attn.shape[1]), lambda i, j: (i, 0)),
            pl.BlockSpec((tm, y_rnn.shape[1]), lambda i, j: (i, 0)),
            pl.BlockSpec((tm, tn), lambda i, j: (i, j)),
            pl.BlockSpec((tm, tn), lambda i, j: (i, j + nn)),
            pl.BlockSpec((w_attn.shape[0], tn), lambda i, j: (0, j)),
            pl.BlockSpec((w_rnn.shape[0], tn), lambda i, j: (0, j)),
            pl.BlockSpec((tn, d), lambda i, j: (j, 0)),
            pl.BlockSpec((tm, d), lambda i, j: (i, 0)),
            pl.BlockSpec((1, d), lambda i, j: (0, 0)),
            pl.BlockSpec((1, d), lambda i, j: (0, 0)),
        ],
        out_specs=pl.BlockSpec((tm, d), lambda i, j: (i, 0)),
        out_shape=jax.ShapeDtypeStruct((m, d), F32),
        compiler_params=pltpu.CompilerParams(
            dimension_semantics=("parallel", "arbitrary"),
            vmem_limit_bytes=VMEM_LIMIT_BYTES),
        name="merge_ln",
    )(y_attn, y_rnn, gw, gw, w_attn, w_rnn, w_out, h, g, b)


def _block_diag_groups(w):
    per = LRU_BLOCKS // LRU_GROUPS
    w4 = w.reshape(LRU_GROUPS, per, LRU_BLOCK, LRU_BLOCK)
    bd = jnp.einsum("gaij,ab->gaibj", w4, jnp.eye(per, dtype=w.dtype))
    return bd.reshape(LRU_GROUPS, LRU_GROUP_W, LRU_GROUP_W)


def kernel(x, positions, ffn1_w_in, ffn1_w_out, ln1_g, ln1_b, w_in, conv_w, conv_b, lru_wa, lru_ba,
           lru_wx, lru_bx, lru_lambda, w_attn_branch, w_rnn_branch, w_out, ln2_g, ln2_b, ffn2_w_in,
           ffn2_w_out, ln3_g, ln3_b):
    batch, seq, d = x.shape
    m = batch * seq
    h = x.reshape(m, d)
    pos_col = positions.reshape(m, 1)
    tab32, tab16 = _rope_tables(pos_col)

    for l in range(DEPTH):
        h = _ffn_ln(h, ffn1_w_in[l].astype(BF16), ffn1_w_out[l].astype(BF16), ln1_g[l][None], ln1_b[l][None])
        hb = h.astype(BF16)

        wl = w_in[l]
        o_q, o_k, o_v = 0, D_MODEL, D_MODEL + N_KV_HEADS * HEAD_DIM
        o_qi = o_v + N_KV_HEADS * HEAD_DIM
        o_ki = o_qi + N_IDX_HEADS * IDX_DIM
        o_wi = o_ki + IDX_DIM
        o_rx = o_wi + N_IDX_HEADS
        o_gt = o_rx + 2 * D_RNN
        zk = jnp.zeros((d, IDX_DIM), wl.dtype)
        w_ki = wl[:, o_ki:o_wi]
        w_qkv = wl[:, o_q:o_qi].astype(BF16)
        w_qik = jnp.concatenate([wl[:, o_qi:o_ki], w_ki, zk, zk, w_ki], axis=1).astype(BF16)
        w_rxg = wl[:, o_rx:o_gt].astype(BF16)
        pad = jnp.zeros((d, 512 - N_IDX_HEADS), wl.dtype)
        w_gw = jnp.concatenate([wl[:, o_gt:], wl[:, o_wi:o_rx], pad], axis=1).astype(BF16)

        qkv = _proj(hb, w_qkv, BF16, tm=1024, tn=512, tab=tab32, rope_shift=ROPE_DIM // 2,
                    n_rope_tiles=(D_MODEL + N_KV_HEADS * HEAD_DIM) // 512,
                    n_scale_tiles=D_MODEL // 512, scale=HEAD_DIM ** -0.5 * LOG2_E, name="proj_qkv")
        qik = _proj(hb, w_qik, BF16, tm=1024, tn=256, tab=tab16, rope_shift=IDX_ROPE_DIM // 2,
                    n_rope_tiles=w_qik.shape[1] // 256, name="proj_idx")
        rxg = _proj(hb, w_rxg, F32, tm=1024, tn=512, name="proj_rnn")
        gw = _proj(hb, w_gw, F32, tm=1024, tn=512, name="proj_gate")

        y_rnn = _rglru(rxg, conv_w[l], conv_b[l][None],
                       _block_diag_groups(lru_wa[l]).astype(BF16), _block_diag_groups(lru_wx[l]).astype(BF16),
                       lru_ba[l][None], lru_bx[l][None], lru_lambda[l][None], batch=batch, seq=seq)
        y_attn = _dsa(qkv, qik, gw, batch=batch, seq=seq)

        h = _merge_ln(y_attn, y_rnn, gw, w_attn_branch[l].astype(BF16), w_rnn_branch[l].astype(BF16),
                      w_out[l].astype(BF16), h, ln2_g[l][None], ln2_b[l][None])
        h = _ffn_ln(h, ffn2_w_in[l].astype(BF16), ffn2_w_out[l].astype(BF16), ln3_g[l][None], ln3_b[l][None])
    return h.reshape(batch, seq, d)
```

```python
import functools

import numpy as np
import jax
import jax.numpy as jnp
from jax import lax
from jax.experimental import pallas as pl
from jax.experimental.pallas import tpu as pltpu

F32 = jnp.float32
BF16 = jnp.bfloat16

D_MODEL = 2048
HEAD_DIM = 128
N_HEADS = D_MODEL // HEAD_DIM
N_KV_HEADS = 4
GROUP = N_HEADS // N_KV_HEADS
ROPE_DIM = HEAD_DIM // 4
ROPE_THETA = 500000.0
N_IDX_HEADS = 16
IDX_DIM = 64
IDX_ROPE_DIM = IDX_DIM // 4
TOPK_MAX = 256
D_RNN = (4 * D_MODEL // 3) // 256 * 256
LRU_BLOCKS = 16
LRU_BLOCK = D_RNN // LRU_BLOCKS
CONV_WIDTH = 4
LRU_C = 8.0
D_FF = (8 * D_MODEL // 3 + 255) // 256 * 256
LN_EPS = 1e-5
DEPTH = 1
DN_ALPHA = (2.0 * DEPTH) ** 0.25

LANES = 128
SUBLANES = 8
VMEM_LIMIT_BYTES = 56 * 1024 * 1024

LRU_GROUPS = 4
LRU_GROUP_W = D_RNN // LRU_GROUPS

INT_MIN = -(2 ** 31)
LOG2_E = 1.4426950408889634


def _layer_norm(y, g, b):
    mu = jnp.mean(y, axis=-1, keepdims=True)
    d = y - mu
    var = jnp.mean(d * d, axis=-1, keepdims=True)
    return d * lax.rsqrt(var + LN_EPS) * g + b


def _dot(a, b):
    return jnp.dot(a, b, preferred_element_type=F32)


def _dot_nt(a, b):
    return lax.dot_general(a, b, (((1,), (1,)), ((), ())), preferred_element_type=F32)


def _ffn_ln_kernel(x_ref, wa_ref, wb_ref, wo_ref, g_ref, b_ref, o_ref, xb_ref):
    j = pl.program_id(1)

    @pl.when(j == 0)
    def _():
        xb_ref[...] = x_ref[...].astype(BF16)
        o_ref[...] = jnp.zeros_like(o_ref)

    xb = xb_ref[...]
    a = _dot(xb, wa_ref[...])
    b = _dot(xb, wb_ref[...])
    act = (jax.nn.silu(a) * b).astype(BF16)
    o_ref[...] += _dot(act, wo_ref[...])

    @pl.when(j == pl.num_programs(1) - 1)
    def _():
        y = DN_ALPHA * x_ref[...] + 0.5 * o_ref[...]
        o_ref[...] = _layer_norm(y, g_ref[...], b_ref[...])


def _ffn_ln(x, w_in, w_out, g, b, *, tm=512, tf=512):
    m, d = x.shape
    f = w_out.shape[0]
    nf = f // tf
    return pl.pallas_call(
        _ffn_ln_kernel,
        grid=(m // tm, nf),
        in_specs=[
            pl.BlockSpec((tm, d), lambda i, j: (i, 0)),
            pl.BlockSpec((d, tf), lambda i, j: (0, j)),
            pl.BlockSpec((d, tf), lambda i, j: (0, j + nf)),
            pl.BlockSpec((tf, d), lambda i, j: (j, 0)),
            pl.BlockSpec((1, d), lambda i, j: (0, 0)),
            pl.BlockSpec((1, d), lambda i, j: (0, 0)),
        ],
        out_specs=pl.BlockSpec((tm, d), lambda i, j: (i, 0)),
        out_shape=jax.ShapeDtypeStruct((m, d), F32),
        scratch_shapes=[pltpu.VMEM((tm, d), BF16)],
        compiler_params=pltpu.CompilerParams(
            dimension_semantics=("parallel", "arbitrary"),
            vmem_limit_bytes=VMEM_LIMIT_BYTES),
        name="ffn_ln",
    )(x, w_in, w_in, w_out, g, b)


def _rope_tab_kernel(pos_ref, f_ref, o32_ref, o16_ref):
    pos = pos_ref[...].astype(F32)
    for row, o_ref in ((0, o32_ref), (3, o16_ref)):
        ang = pos * f_ref[row:row + 1, :]
        s = jnp.sin(ang)
        o_ref[0] = jnp.cos(ang)
        o_ref[1] = s * f_ref[row + 1:row + 2, :]
        o_ref[2] = s * f_ref[row + 2:row + 3, :]


def _rope_lane_table():
    lane = np.arange(LANES)

    def rows(rot_dim, period):
        half = rot_dim // 2
        inv_freq = jnp.power(ROPE_THETA, -(jnp.arange(half, dtype=F32) * 2.0 / rot_dim))
        l = lane % period
        freq = jnp.where(l < rot_dim, inv_freq[l % half], 0.0)
        plus = ((l >= half) & (l < rot_dim)).astype(np.float32)
        minus = -(l < half).astype(np.float32)
        return [freq, jnp.asarray(plus), jnp.asarray(minus)]

    z = jnp.zeros((LANES,), F32)
    return jnp.stack(rows(ROPE_DIM, HEAD_DIM) + rows(IDX_ROPE_DIM, IDX_DIM) + [z, z]).astype(F32)


def _rope_tables(pos_col, *, tm=1024):
    m = pos_col.shape[0]
    shp = jax.ShapeDtypeStruct((3, m, LANES), F32)
    return pl.pallas_call(
        _rope_tab_kernel,
        grid=(m // tm,),
        in_specs=[pl.BlockSpec((tm, 1), lambda i: (i, 0)),
                  pl.BlockSpec((8, LANES), lambda i: (0, 0))],
        out_specs=[pl.BlockSpec((3, tm, LANES), lambda i: (0, i, 0)),
                   pl.BlockSpec((3, tm, LANES), lambda i: (0, i, 0))],
        out_shape=[shp, shp],
        compiler_params=pltpu.CompilerParams(dimension_semantics=("parallel",)),
        name="rope_tables",
    )(pos_col, _rope_lane_table())


def _proj_kernel(x_ref, w_ref, *rest, tn, rope_shift, n_rope_tiles, n_scale_tiles, scale):
    if rope_shift is None:
        (o_ref,) = rest
        o_ref[...] = _dot(x_ref[...], w_ref[...]).astype(o_ref.dtype)
        return
    tab_ref, o_ref = rest
    j = pl.program_id(1)
    acc = _dot(x_ref[...], w_ref[...])

    @pl.when(j < n_rope_tiles)
    def _():
        c, s_hi, s_lo = tab_ref[0], tab_ref[1], tab_ref[2]
        sc = jnp.where(j < n_scale_tiles, scale, 1.0).astype(F32)
        for t in range(tn // LANES):
            a = acc[:, t * LANES:(t + 1) * LANES]
            r = (a * c + pltpu.roll(a, rope_shift, 1) * s_hi
                 + pltpu.roll(a, LANES - rope_shift, 1) * s_lo)
            o_ref[:, t * LANES:(t + 1) * LANES] = (r * sc).astype(o_ref.dtype)

    @pl.when(j >= n_rope_tiles)
    def _():
        o_ref[...] = acc.astype(o_ref.dtype)


def _proj(xb, w, out_dtype, *, tm, tn, tab=None, rope_shift=None, n_rope_tiles=0,
          n_scale_tiles=0, scale=1.0, name="proj"):
    m, d = xb.shape
    n = w.shape[1]
    in_specs = [pl.BlockSpec((tm, d), lambda i, j: (i, 0)),
                pl.BlockSpec((d, tn), lambda i, j: (0, j))]
    args = [xb, w]
    if rope_shift is not None:
        in_specs.append(pl.BlockSpec((3, tm, LANES), lambda i, j: (0, i, 0)))
        args.append(tab)
    return pl.pallas_call(
        functools.partial(_proj_kernel, tn=tn, rope_shift=rope_shift, n_rope_tiles=n_rope_tiles,
                          n_scale_tiles=n_scale_tiles, scale=scale),
        grid=(m // tm, n // tn),
        in_specs=in_specs,
        out_specs=pl.BlockSpec((tm, tn), lambda i, j: (i, j)),
        out_shape=jax.ShapeDtypeStruct((m, n), out_dtype),
        compiler_params=pltpu.CompilerParams(
            dimension_semantics=("parallel", "arbitrary"),
            vmem_limit_bytes=VMEM_LIMIT_BYTES),
        name=name,
    )(*args)


def _rglru_kernel(rx_ref, rg_ref, cw_ref, cb_ref, wa_ref, wx_ref, ba_ref, bx_ref, lam_ref,
                  o_ref, xbuf, a_s, b_s, h_s, *, tt):
    t = pl.program_id(2)
    w = xbuf.shape[1]

    @pl.when(t == 0)
    def _():
        xbuf[0:SUBLANES, :] = jnp.zeros((SUBLANES, w), F32)
        h_s[...] = jnp.zeros_like(h_s)

    xbuf[SUBLANES:SUBLANES + tt, :] = rx_ref[...]
    xc = cb_ref[...]
    for k in range(CONV_WIDTH):
        off = SUBLANES - (CONV_WIDTH - 1) + k
        xc = xc + cw_ref[k:k + 1, :] * xbuf[off:off + tt, :]
    xbuf[0:SUBLANES, :] = rx_ref[tt - SUBLANES:tt, :]

    xcb = xc.astype(BF16)
    r = jax.nn.sigmoid(_dot(xcb, wa_ref[0]) + ba_ref[...])
    ig = jax.nn.sigmoid(_dot(xcb, wx_ref[0]) + bx_ref[...])
    log_a = -LRU_C * r * jax.nn.softplus(-lam_ref[...])
    a = jnp.exp(log_a)
    a_s[...] = a
    b_s[...] = jnp.sqrt(-jnp.tanh(log_a) * (a * a + 1.0)) * (ig * xc)

    row = lax.broadcasted_iota(jnp.int32, (SUBLANES, w), 0)

    def body(g, h):
        sl = pl.ds(pl.multiple_of(g * SUBLANES, SUBLANES), SUBLANES)
        a8 = a_s[sl, :]
        b8 = b_s[sl, :]
        for d in (1, 2, 4):
            keep = row >= d
            a_sh = jnp.where(keep, pltpu.roll(a8, d, 0), 1.0)
            b_sh = jnp.where(keep, pltpu.roll(b8, d, 0), 0.0)
            b8 = a8 * b_sh + b8
            a8 = a8 * a_sh
        hrows = a8 * h + b8
        b_s[sl, :] = hrows
        return jnp.broadcast_to(hrows[SUBLANES - 1:SUBLANES, :], (SUBLANES, w))

    h_s[...] = lax.fori_loop(0, tt // SUBLANES, body, h_s[...])
    o_ref[...] = (b_s[...] * jax.nn.gelu(rg_ref[...])).astype(o_ref.dtype)


def _rglru(rxg, conv_w, conv_b, wa_bd, wx_bd, ba, bx, lam, *, batch, seq, tt=512):
    m = rxg.shape[0]
    nt = seq // tt
    gw = LRU_GROUP_W
    vec = lambda: pl.BlockSpec((1, gw), lambda b, g, t: (0, g))
    return pl.pallas_call(
        functools.partial(_rglru_kernel, tt=tt),
        grid=(batch, LRU_GROUPS, nt),
        in_specs=[
            pl.BlockSpec((tt, gw), lambda b, g, t: (b * nt + t, g)),
            pl.BlockSpec((tt, gw), lambda b, g, t: (b * nt + t, g + LRU_GROUPS)),
            pl.BlockSpec((CONV_WIDTH, gw), lambda b, g, t: (0, g)),
            vec(),
            pl.BlockSpec((1, gw, gw), lambda b, g, t: (g, 0, 0)),
            pl.BlockSpec((1, gw, gw), lambda b, g, t: (g, 0, 0)),
            vec(), vec(), vec(),
        ],
        out_specs=pl.BlockSpec((tt, gw), lambda b, g, t: (b * nt + t, g)),
        out_shape=jax.ShapeDtypeStruct((m, D_RNN), BF16),
        scratch_shapes=[pltpu.VMEM((tt + SUBLANES, gw), F32), pltpu.VMEM((tt, gw), F32),
                        pltpu.VMEM((tt, gw), F32), pltpu.VMEM((SUBLANES, gw), F32)],
        compiler_params=pltpu.CompilerParams(
            dimension_semantics=("parallel", "parallel", "arbitrary"),
            vmem_limit_bytes=VMEM_LIMIT_BYTES),
        name="rglru",
    )(rxg, rxg, conv_w, conv_b, wa_bd, wx_bd, ba, bx, lam)


def _dsa_kernel(qi_ref, kie_ref, kio_ref, wi_ref, q_ref, k_ref, v_ref, prev_ref, o_ref,
                score_ref, bias_ref, *, tq, ck, topk, s_eff, tile0):
    del prev_ref
    i = tile0 + pl.program_id(1)
    n_pair = N_IDX_HEADS // 2

    qi = qi_ref[...]
    qs = jnp.concatenate([qi[:, j * LANES:(j + 1) * LANES] for j in range(n_pair)], axis=0)
    w = wi_ref[...] * (N_IDX_HEADS ** -0.5 * IDX_DIM ** -0.5)
    for c in range(s_eff // ck):
        le = _dot_nt(qs, kie_ref[c * ck:(c + 1) * ck, :])
        lo = _dot_nt(qs, kio_ref[c * ck:(c + 1) * ck, :])
        acc = jnp.zeros((tq, ck), F32)
        for j in range(n_pair):
            acc = acc + jnp.maximum(le[j * tq:(j + 1) * tq], 0.0) * w[:, 2 * j:2 * j + 1]
            acc = acc + jnp.maximum(lo[j * tq:(j + 1) * tq], 0.0) * w[:, 2 * j + 1:2 * j + 2]
        score_ref[:, c * ck:(c + 1) * ck] = acc

    qpos = i * tq + lax.broadcasted_iota(jnp.int32, (tq, s_eff), 0)
    kpos = lax.broadcasted_iota(jnp.int32, (tq, s_eff), 1)
    score = jnp.where(kpos <= qpos, score_ref[...], -jnp.inf)

    def key_to_float(key):
        return pltpu.bitcast(jnp.where(key < 0, key ^ 0x7FFFFFFF, key), F32)

    def search(it, key):
        cand = key ^ lax.shift_left(jnp.int32(1), 31 - it)
        cnt = jnp.sum(jnp.where(score >= key_to_float(cand), 1.0, 0.0), axis=1, keepdims=True)
        return jnp.where(cnt >= topk, cand, key)

    key = lax.fori_loop(0, 32, search, jnp.full((tq, 1), INT_MIN, jnp.int32))
    thr = key_to_float(jnp.maximum(key, INT_MIN + 0x00800000))
    ge = score >= thr
    cnt = jnp.sum(jnp.where(ge, 1.0, 0.0), axis=1, keepdims=True)
    bias_ref[...] = jnp.where(ge, 0.0, -jnp.inf).astype(F32)

    @pl.when(jnp.max(cnt) > topk)
    def _():
        tw = 2 * LANES
        need = topk - jnp.sum(jnp.where(score > thr, 1.0, 0.0), axis=1, keepdims=True)
        tri = jnp.where(lax.broadcasted_iota(jnp.int32, (tw, tw), 0)
                        <= lax.broadcasted_iota(jnp.int32, (tw, tw), 1), 1.0, 0.0).astype(BF16)
        carry = jnp.zeros((tq, 1), F32)
        for c in range(s_eff // tw):
            kc = score[:, c * tw:(c + 1) * tw]
            eqc = jnp.where(kc == thr, 1.0, 0.0)
            rank = _dot(eqc.astype(BF16), tri) + carry
            take = (kc > thr) | ((kc == thr) & (rank <= need))
            bias_ref[:, c * tw:(c + 1) * tw] = jnp.where(take, 0.0, -jnp.inf).astype(F32)
            carry = carry + jnp.sum(eqc, axis=1, keepdims=True)

    bias = bias_ref[...]
    q = q_ref[...]
    ones = jnp.ones((s_eff, HEAD_DIM), BF16)
    for g in range(N_KV_HEADS):
        qg = jnp.concatenate(
            [q[:, (g * GROUP + h) * HEAD_DIM:(g * GROUP + h + 1) * HEAD_DIM] for h in range(GROUP)],
            axis=0)
        s = _dot_nt(qg, k_ref[:s_eff, g * HEAD_DIM:(g + 1) * HEAD_DIM])
        s = s.reshape(GROUP, tq, s_eff) + bias[None]
        p = jnp.exp2(s - jnp.max(s, axis=-1, keepdims=True)).astype(BF16)
        v_aug = jnp.concatenate([v_ref[:s_eff, g * HEAD_DIM:(g + 1) * HEAD_DIM], ones], axis=1)
        o = _dot(p.reshape(GROUP * tq, s_eff), v_aug)
        o = o[:, :HEAD_DIM] / o[:, HEAD_DIM:]
        for h in range(GROUP):
            col = (g * GROUP + h) * HEAD_DIM
            o_ref[:, col:col + HEAD_DIM] = o[h * tq:(h + 1) * tq].astype(o_ref.dtype)


def _dsa(qkv, qik, gw, *, batch, seq, tq=128, ck=512):
    m = qkv.shape[0]
    nq = seq // tq
    per = ck // tq
    topk = min(TOPK_MAX, seq // 4)
    kv_w = N_KV_HEADS * HEAD_DIM
    qi_w = N_IDX_HEADS * IDX_DIM
    y = jnp.zeros((m, D_MODEL), BF16)
    for n in range(1, seq // ck + 1):
        s_eff, tile0 = n * ck, (n - 1) * per
        row = lambda b, i, tile0=tile0: b * nq + tile0 + i
        y = pl.pallas_call(
            functools.partial(_dsa_kernel, tq=tq, ck=ck, topk=topk, s_eff=s_eff, tile0=tile0),
            grid=(batch, per),
            in_specs=[
                pl.BlockSpec((tq, qi_w), lambda b, i, row=row: (row(b, i), 0)),
                pl.BlockSpec((seq, LANES), lambda b, i: (b, qi_w // LANES)),
                pl.BlockSpec((seq, LANES), lambda b, i: (b, qi_w // LANES + 1)),
                pl.BlockSpec((tq, LANES), lambda b, i, row=row: (row(b, i), 2 * D_MODEL // LANES)),
                pl.BlockSpec((tq, D_MODEL), lambda b, i, row=row: (row(b, i), 0)),
                pl.BlockSpec((seq, kv_w), lambda b, i: (b, D_MODEL // kv_w)),
                pl.BlockSpec((seq, kv_w), lambda b, i: (b, D_MODEL // kv_w + 1)),
                pl.BlockSpec(memory_space=pl.ANY),
            ],
            out_specs=pl.BlockSpec((tq, D_MODEL), lambda b, i, row=row: (row(b, i), 0)),
            out_shape=jax.ShapeDtypeStruct((m, D_MODEL), BF16),
            input_output_aliases={7: 0},
            scratch_shapes=[pltpu.VMEM((tq, s_eff), F32), pltpu.VMEM((tq, s_eff), F32)],
            compiler_params=pltpu.CompilerParams(
                dimension_semantics=("parallel", "arbitrary"),
                vmem_limit_bytes=VMEM_LIMIT_BYTES),
            name=f"dsa_{s_eff}",
        )(qik, qik, qik, gw, qkv, qkv, qkv, y)
    return y


def _merge_ln_kernel(ya_ref, yr_ref, ga_ref, gr_ref, wa_ref, wr_ref, wo_ref, h_ref, g_ref, b_ref, o_ref):
    j = pl.program_id(1)

    @pl.when(j == 0)
    def _():
        o_ref[...] = jnp.zeros_like(o_ref)

    merged = (jax.nn.sigmoid(ga_ref[...]) * _dot(ya_ref[...], wa_ref[...])
              + jax.nn.sigmoid(gr_ref[...]) * _dot(yr_ref[...], wr_ref[...]))
    o_ref[...] += _dot(merged.astype(BF16), wo_ref[...])

    @pl.when(j == pl.num_programs(1) - 1)
    def _():
        y = DN_ALPHA * h_ref[...] + o_ref[...]
        o_ref[...] = _layer_norm(y, g_ref[...], b_ref[...])


def _merge_ln(y_attn, y_rnn, gw, w_attn, w_rnn, w_out, h, g, b, *, tm=512, tn=512):
    m, d = h.shape
    nn = d // tn
    return pl.pallas_call(
        _merge_ln_kernel,
        grid=(m // tm, nn),
        in_specs=[
            pl.BlockSpec((tm, y_attn.shape[1]), lambda i, j: (i, 0)),
            pl.BlockSpec((tm, y_rnn.shape[1]), lambda i, j: (i, 0)),
            pl.BlockSpec((tm, tn), lambda i, j: (i, j)),
            pl.BlockSpec((tm, tn), lambda i, j: (i, j + nn)),
            pl.BlockSpec((w_attn.shape[0], tn), lambda i, j: (0, j)),
            pl.BlockSpec((w_rnn.shape[0], tn), lambda i, j: (0, j)),
            pl.BlockSpec((tn, d), lambda i, j: (j, 0)),
            pl.BlockSpec((tm, d), lambda i, j: (i, 0)),
            pl.BlockSpec((1, d), lambda i, j: (0, 0)),
            pl.BlockSpec((1, d), lambda i, j: (0, 0)),
        ],
        out_specs=pl.BlockSpec((tm, d), lambda i, j: (i, 0)),
        out_shape=jax.ShapeDtypeStruct((m, d), F32),
        compiler_params=pltpu.CompilerParams(
            dimension_semantics=("parallel", "arbitrary"),
            vmem_limit_bytes=VMEM_LIMIT_BYTES),
        name="merge_ln",
    )(y_attn, y_rnn, gw, gw, w_attn, w_rnn, w_out, h, g, b)


def _block_diag_groups(w):
    per = LRU_BLOCKS // LRU_GROUPS
    w4 = w.reshape(LRU_GROUPS, per, LRU_BLOCK, LRU_BLOCK)
    bd = jnp.einsum("gaij,ab->gaibj", w4, jnp.eye(per, dtype=w.dtype))
    return bd.reshape(LRU_GROUPS, LRU_GROUP_W, LRU_GROUP_W)


def kernel(x, positions, ffn1_w_in, ffn1_w_out, ln1_g, ln1_b, w_in, conv_w, conv_b, lru_wa, lru_ba,
           lru_wx, lru_bx, lru_lambda, w_attn_branch, w_rnn_branch, w_out, ln2_g, ln2_b, ffn2_w_in,
           ffn2_w_out, ln3_g, ln3_b):
    batch, seq, d = x.shape
    m = batch * seq
    h = x.reshape(m, d)
    pos_col = positions.reshape(m, 1)
    tab32, tab16 = _rope_tables(pos_col)

    for l in range(DEPTH):
        h = _ffn_ln(h, ffn1_w_in[l].astype(BF16), ffn1_w_out[l].astype(BF16), ln1_g[l][None], ln1_b[l][None])
        hb = h.astype(BF16)

        wl = w_in[l]
        o_q, o_k, o_v = 0, D_MODEL, D_MODEL + N_KV_HEADS * HEAD_DIM
        o_qi = o_v + N_KV_HEADS * HEAD_DIM
        o_ki = o_qi + N_IDX_HEADS * IDX_DIM
        o_wi = o_ki + IDX_DIM
        o_rx = o_wi + N_IDX_HEADS
        o_gt = o_rx + 2 * D_RNN
        zk = jnp.zeros((d, IDX_DIM), wl.dtype)
        w_ki = wl[:, o_ki:o_wi]
        w_qkv = wl[:, o_q:o_qi].astype(BF16)
        w_qik = jnp.concatenate([wl[:, o_qi:o_ki], w_ki, zk, zk, w_ki], axis=1).astype(BF16)
        w_rxg = wl[:, o_rx:o_gt].astype(BF16)
        pad = jnp.zeros((d, 512 - N_IDX_HEADS), wl.dtype)
        w_gw = jnp.concatenate([wl[:, o_gt:], wl[:, o_wi:o_rx], pad], axis=1).astype(BF16)

        qkv = _proj(hb, w_qkv, BF16, tm=1024, tn=512, tab=tab32, rope_shift=ROPE_DIM // 2,
                    n_rope_tiles=(D_MODEL + N_KV_HEADS * HEAD_DIM) // 512,
                    n_scale_tiles=D_MODEL // 512, scale=HEAD_DIM ** -0.5 * LOG2_E, name="proj_qkv")
        qik = _proj(hb, w_qik, BF16, tm=1024, tn=256, tab=tab16, rope_shift=IDX_ROPE_DIM // 2,
                    n_rope_tiles=w_qik.shape[1] // 256, name="proj_idx")
        rxg = _proj(hb, w_rxg, F32, tm=1024, tn=512, name="proj_rnn")
        gw = _proj(hb, w_gw, F32, tm=1024, tn=512, name="proj_gate")

        y_rnn = _rglru(rxg, conv_w[l], conv_b[l][None],
                       _block_diag_groups(lru_wa[l]).astype(BF16), _block_diag_groups(lru_wx[l]).astype(BF16),
                       lru_ba[l][None], lru_bx[l][None], lru_lambda[l][None], batch=batch, seq=seq)
        y_attn = _dsa(qkv, qik, gw, batch=batch, seq=seq)

        h = _merge_ln(y_attn, y_rnn, gw, w_attn_branch[l].astype(BF16), w_rnn_branch[l].astype(BF16),
                      w_out[l].astype(BF16), h, ln2_g[l][None], ln2_b[l][None])
        h = _ffn_ln(h, ffn2_w_in[l].astype(BF16), ffn2_w_out[l].astype(BF16), ln3_g[l][None], ln3_b[l][None])
    return h.reshape(batch, seq, d)
```

```python
import functools

import numpy as np
import jax
import jax.numpy as jnp
from jax import lax
from jax.experimental import pallas as pl
from jax.experimental.pallas import tpu as pltpu

F32 = jnp.float32
BF16 = jnp.bfloat16

D_MODEL = 2048
HEAD_DIM = 128
N_HEADS = D_MODEL // HEAD_DIM
N_KV_HEADS = 4
GROUP = N_HEADS // N_KV_HEADS
ROPE_DIM = HEAD_DIM // 4
ROPE_THETA = 500000.0
N_IDX_HEADS = 16
IDX_DIM = 64
IDX_ROPE_DIM = IDX_DIM // 4
TOPK_MAX = 256
D_RNN = (4 * D_MODEL // 3) // 256 * 256
LRU_BLOCKS = 16
LRU_BLOCK = D_RNN // LRU_BLOCKS
CONV_WIDTH = 4
LRU_C = 8.0
D_FF = (8 * D_MODEL // 3 + 255) // 256 * 256
LN_EPS = 1e-5
DEPTH = 1
DN_ALPHA = (2.0 * DEPTH) ** 0.25

LANES = 128
SUBLANES = 8
VMEM_LIMIT_BYTES = 56 * 1024 * 1024

LRU_GROUPS = 4
LRU_GROUP_W = D_RNN // LRU_GROUPS

INT_MIN = -(2 ** 31)
LOG2_E = 1.4426950408889634


def _layer_norm(y, g, b):
    mu = jnp.mean(y, axis=-1, keepdims=True)
    d = y - mu
    var = jnp.mean(d * d, axis=-1, keepdims=True)
    return d * lax.rsqrt(var + LN_EPS) * g + b


def _dot(a, b):
    return jnp.dot(a, b, preferred_element_type=F32)


def _dot_nt(a, b):
    return lax.dot_general(a, b, (((1,), (1,)), ((), ())), preferred_element_type=F32)


def _ffn_ln_kernel(x_ref, wa_ref, wb_ref, wo_ref, g_ref, b_ref, o_ref, *rest):
    xb_ref = rest[-1]
    j = pl.program_id(1)

    @pl.when(j == 0)
    def _():
        xb_ref[...] = x_ref[...].astype(BF16)
        o_ref[...] = jnp.zeros_like(o_ref)

    xb = xb_ref[...]
    a = _dot(xb, wa_ref[...])
    b = _dot(xb, wb_ref[...])
    act = (jax.nn.silu(a) * b).astype(BF16)
    o_ref[...] += _dot(act, wo_ref[...])

    @pl.when(j == pl.num_programs(1) - 1)
    def _():
        y = DN_ALPHA * x_ref[...] + 0.5 * o_ref[...]
        out = _layer_norm(y, g_ref[...], b_ref[...])
        o_ref[...] = out
        if len(rest) == 2:
            rest[0][...] = out.astype(BF16)


def _ffn_ln(x, w_in, w_out, g, b, *, emit_bf16, tm=512, tf=512):
    m, d = x.shape
    f = w_out.shape[0]
    nf = f // tf
    out_spec = pl.BlockSpec((tm, d), lambda i, j: (i, 0))
    out_specs, out_shape = out_spec, jax.ShapeDtypeStruct((m, d), F32)
    if emit_bf16:
        out_specs, out_shape = [out_spec, out_spec], [out_shape, jax.ShapeDtypeStruct((m, d), BF16)]
    return pl.pallas_call(
        _ffn_ln_kernel,
        grid=(m // tm, nf),
        in_specs=[
            pl.BlockSpec((tm, d), lambda i, j: (i, 0)),
            pl.BlockSpec((d, tf), lambda i, j: (0, j)),
            pl.BlockSpec((d, tf), lambda i, j: (0, j + nf)),
            pl.BlockSpec((tf, d), lambda i, j: (j, 0)),
            pl.BlockSpec((1, d), lambda i, j: (0, 0)),
            pl.BlockSpec((1, d), lambda i, j: (0, 0)),
        ],
        out_specs=out_specs,
        out_shape=out_shape,
        scratch_shapes=[pltpu.VMEM((tm, d), BF16)],
        compiler_params=pltpu.CompilerParams(
            dimension_semantics=("parallel", "arbitrary"),
            vmem_limit_bytes=VMEM_LIMIT_BYTES),
        name="ffn_ln",
    )(x, w_in, w_in, w_out, g, b)


def _rope_tab_kernel(pos_ref, f_ref, o32_ref, o16_ref):
    pos = pos_ref[...].astype(F32)
    for row, o_ref in ((0, o32_ref), (3, o16_ref)):
        ang = pos * f_ref[row:row + 1, :]
        s = jnp.sin(ang)
        o_ref[0] = jnp.cos(ang)
        o_ref[1] = s * f_ref[row + 1:row + 2, :]
        o_ref[2] = s * f_ref[row + 2:row + 3, :]


def _rope_lane_table():
    lane = np.arange(LANES)

    def rows(rot_dim, period):
        half = rot_dim // 2
        inv_freq = jnp.power(ROPE_THETA, -(jnp.arange(half, dtype=F32) * 2.0 / rot_dim))
        l = lane % period
        freq = jnp.where(l < rot_dim, inv_freq[l % half], 0.0)
        plus = ((l >= half) & (l < rot_dim)).astype(np.float32)
        minus = -(l < half).astype(np.float32)
        return [freq, jnp.asarray(plus), jnp.asarray(minus)]

    z = jnp.zeros((LANES,), F32)
    return jnp.stack(rows(ROPE_DIM, HEAD_DIM) + rows(IDX_ROPE_DIM, IDX_DIM) + [z, z]).astype(F32)


def _rope_tables(pos_col, *, tm=1024):
    m = pos_col.shape[0]
    shp = jax.ShapeDtypeStruct((3, m, LANES), F32)
    return pl.pallas_call(
        _rope_tab_kernel,
        grid=(m // tm,),
        in_specs=[pl.BlockSpec((tm, 1), lambda i: (i, 0)),
                  pl.BlockSpec((8, LANES), lambda i: (0, 0))],
        out_specs=[pl.BlockSpec((3, tm, LANES), lambda i: (0, i, 0)),
                   pl.BlockSpec((3, tm, LANES), lambda i: (0, i, 0))],
        out_shape=[shp, shp],
        compiler_params=pltpu.CompilerParams(dimension_semantics=("parallel",)),
        name="rope_tables",
    )(pos_col, _rope_lane_table())


def _proj_kernel(x_ref, w_ref, *rest, tn, rope_shift, n_rope_tiles, n_scale_tiles, scale):
    if rope_shift is None:
        (o_ref,) = rest
        o_ref[...] = _dot(x_ref[...], w_ref[...]).astype(o_ref.dtype)
        return
    tab_ref, o_ref = rest
    j = pl.program_id(1)
    acc = _dot(x_ref[...], w_ref[...])

    @pl.when(j < n_rope_tiles)
    def _():
        c, s_hi, s_lo = tab_ref[0], tab_ref[1], tab_ref[2]
        sc = jnp.where(j < n_scale_tiles, scale, 1.0).astype(F32)
        for t in range(tn // LANES):
            a = acc[:, t * LANES:(t + 1) * LANES]
            r = (a * c + pltpu.roll(a, rope_shift, 1) * s_hi
                 + pltpu.roll(a, LANES - rope_shift, 1) * s_lo)
            o_ref[:, t * LANES:(t + 1) * LANES] = (r * sc).astype(o_ref.dtype)

    @pl.when(j >= n_rope_tiles)
    def _():
        o_ref[...] = acc.astype(o_ref.dtype)


def _proj(xb, w, out_dtype, *, tm, tn, tab=None, rope_shift=None, n_rope_tiles=0,
          n_scale_tiles=0, scale=1.0, name="proj"):
    m, d = xb.shape
    n = w.shape[1]
    in_specs = [pl.BlockSpec((tm, d), lambda i, j: (i, 0)),
                pl.BlockSpec((d, tn), lambda i, j: (0, j))]
    args = [xb, w]
    if rope_shift is not None:
        in_specs.append(pl.BlockSpec((3, tm, LANES), lambda i, j: (0, i, 0)))
        args.append(tab)
    return pl.pallas_call(
        functools.partial(_proj_kernel, tn=tn, rope_shift=rope_shift, n_rope_tiles=n_rope_tiles,
                          n_scale_tiles=n_scale_tiles, scale=scale),
        grid=(m // tm, n // tn),
        in_specs=in_specs,
        out_specs=pl.BlockSpec((tm, tn), lambda i, j: (i, j)),
        out_shape=jax.ShapeDtypeStruct((m, n), out_dtype),
        compiler_params=pltpu.CompilerParams(
            dimension_semantics=("parallel", "arbitrary"),
            vmem_limit_bytes=VMEM_LIMIT_BYTES),
        name=name,
    )(*args)


def _rglru_kernel(rx_ref, rg_ref, cw_ref, cb_ref, wa_ref, wx_ref, ba_ref, bx_ref, lam_ref,
                  o_ref, xbuf, a_s, b_s, h_s, *, tt):
    t = pl.program_id(2)
    w = xbuf.shape[1]

    @pl.when(t == 0)
    def _():
        xbuf[0:SUBLANES, :] = jnp.zeros((SUBLANES, w), F32)
        h_s[...] = jnp.zeros_like(h_s)

    xbuf[SUBLANES:SUBLANES + tt, :] = rx_ref[...]
    xc = cb_ref[...]
    for k in range(CONV_WIDTH):
        off = SUBLANES - (CONV_WIDTH - 1) + k
        xc = xc + cw_ref[k:k + 1, :] * xbuf[off:off + tt, :]
    xbuf[0:SUBLANES, :] = rx_ref[tt - SUBLANES:tt, :]

    xcb = xc.astype(BF16)
    r = jax.nn.sigmoid(_dot(xcb, wa_ref[0]) + ba_ref[...])
    ig = jax.nn.sigmoid(_dot(xcb, wx_ref[0]) + bx_ref[...])
    log_a = -LRU_C * r * jax.nn.softplus(-lam_ref[...])
    a = jnp.exp(log_a)
    a_s[...] = a
    b_s[...] = jnp.sqrt(-jnp.tanh(log_a) * (a * a + 1.0)) * (ig * xc)

    row = lax.broadcasted_iota(jnp.int32, (SUBLANES, w), 0)

    def body(g, h):
        sl = pl.ds(pl.multiple_of(g * SUBLANES, SUBLANES), SUBLANES)
        a8 = a_s[sl, :]
        b8 = b_s[sl, :]
        for d in (1, 2, 4):
            keep = row >= d
            a_sh = jnp.where(keep, pltpu.roll(a8, d, 0), 1.0)
            b_sh = jnp.where(keep, pltpu.roll(b8, d, 0), 0.0)
            b8 = a8 * b_sh + b8
            a8 = a8 * a_sh
        hrows = a8 * h + b8
        b_s[sl, :] = hrows
        return jnp.broadcast_to(hrows[SUBLANES - 1:SUBLANES, :], (SUBLANES, w))

    h_s[...] = lax.fori_loop(0, tt // SUBLANES, body, h_s[...])
    o_ref[...] = (b_s[...] * jax.nn.gelu(rg_ref[...])).astype(o_ref.dtype)


def _rglru(rxg, conv_w, conv_b, wa_bd, wx_bd, ba, bx, lam, *, batch, seq, tt=512):
    m = rxg.shape[0]
    nt = seq // tt
    gw = LRU_GROUP_W
    vec = lambda: pl.BlockSpec((1, gw), lambda b, g, t: (0, g))
    return pl.pallas_call(
        functools.partial(_rglru_kernel, tt=tt),
        grid=(batch, LRU_GROUPS, nt),
        in_specs=[
            pl.BlockSpec((tt, gw), lambda b, g, t: (b * nt + t, g)),
            pl.BlockSpec((tt, gw), lambda b, g, t: (b * nt + t, g + LRU_GROUPS)),
            pl.BlockSpec((CONV_WIDTH, gw), lambda b, g, t: (0, g)),
            vec(),
            pl.BlockSpec((1, gw, gw), lambda b, g, t: (g, 0, 0)),
            pl.BlockSpec((1, gw, gw), lambda b, g, t: (g, 0, 0)),
            vec(), vec(), vec(),
        ],
        out_specs=pl.BlockSpec((tt, gw), lambda b, g, t: (b * nt + t, g)),
        out_shape=jax.ShapeDtypeStruct((m, D_RNN), BF16),
        scratch_shapes=[pltpu.VMEM((tt + SUBLANES, gw), F32), pltpu.VMEM((tt, gw), F32),
                        pltpu.VMEM((tt, gw), F32), pltpu.VMEM((SUBLANES, gw), F32)],
        compiler_params=pltpu.CompilerParams(
            dimension_semantics=("parallel", "parallel", "arbitrary"),
            vmem_limit_bytes=VMEM_LIMIT_BYTES),
        name="rglru",
    )(rxg, rxg, conv_w, conv_b, wa_bd, wx_bd, ba, bx, lam)


def _dsa_kernel(qi_ref, kie_ref, kio_ref, wi_ref, q_ref, k_ref, v_ref, prev_ref, o_ref,
                score_ref, bias_ref, *, tq, ck, topk, s_eff, tile0):
    del prev_ref
    i = tile0 + pl.program_id(1)
    n_pair = N_IDX_HEADS // 2

    qi = qi_ref[...]
    qs = jnp.concatenate([qi[:, j * LANES:(j + 1) * LANES] for j in range(n_pair)], axis=0)
    w = wi_ref[...] * (N_IDX_HEADS ** -0.5 * IDX_DIM ** -0.5)
    for c in range(s_eff // ck):
        le = _dot_nt(qs, kie_ref[c * ck:(c + 1) * ck, :])
        lo = _dot_nt(qs, kio_ref[c * ck:(c + 1) * ck, :])
        acc = jnp.zeros((tq, ck), F32)
        for j in range(n_pair):
            acc = acc + jnp.maximum(le[j * tq:(j + 1) * tq], 0.0) * w[:, 2 * j:2 * j + 1]
            acc = acc + jnp.maximum(lo[j * tq:(j + 1) * tq], 0.0) * w[:, 2 * j + 1:2 * j + 2]
        score_ref[:, c * ck:(c + 1) * ck] = acc

    qpos = i * tq + lax.broadcasted_iota(jnp.int32, (tq, s_eff), 0)
    kpos = lax.broadcasted_iota(jnp.int32, (tq, s_eff), 1)
    score = jnp.where(kpos <= qpos, score_ref[...], -jnp.inf)

    def key_to_float(key):
        return pltpu.bitcast(jnp.where(key < 0, key ^ 0x7FFFFFFF, key), F32)

    def search(it, key):
        cand = key ^ lax.shift_left(jnp.int32(1), 31 - it)
        cnt = jnp.sum(jnp.where(score >= key_to_float(cand), 1.0, 0.0), axis=1, keepdims=True)
        return jnp.where(cnt >= topk, cand, key)

    key = lax.fori_loop(0, 32, search, jnp.full((tq, 1), INT_MIN, jnp.int32))
    thr = key_to_float(jnp.maximum(key, INT_MIN + 0x00800000))
    ge = score >= thr
    cnt = jnp.sum(jnp.where(ge, 1.0, 0.0), axis=1, keepdims=True)
    bias_ref[...] = jnp.where(ge, 0.0, -jnp.inf).astype(F32)

    @pl.when(jnp.max(cnt) > topk)
    def _():
        tw = 2 * LANES
        need = topk - jnp.sum(jnp.where(score > thr, 1.0, 0.0), axis=1, keepdims=True)
        tri = jnp.where(lax.broadcasted_iota(jnp.int32, (tw, tw), 0)
                        <= lax.broadcasted_iota(jnp.int32, (tw, tw), 1), 1.0, 0.0).astype(BF16)
        carry = jnp.zeros((tq, 1), F32)
        for c in range(s_eff // tw):
            kc = score[:, c * tw:(c + 1) * tw]
            eqc = jnp.where(kc == thr, 1.0, 0.0)
            rank = _dot(eqc.astype(BF16), tri) + carry
            take = (kc > thr) | ((kc == thr) & (rank <= need))
            bias_ref[:, c * tw:(c + 1) * tw] = jnp.where(take, 0.0, -jnp.inf).astype(F32)
            carry = carry + jnp.sum(eqc, axis=1, keepdims=True)

    bias = bias_ref[...]
    q = q_ref[...]
    ones = jnp.ones((s_eff, HEAD_DIM), BF16)
    for g in range(N_KV_HEADS):
        qg = jnp.concatenate(
            [q[:, (g * GROUP + h) * HEAD_DIM:(g * GROUP + h + 1) * HEAD_DIM] for h in range(GROUP)],
            axis=0)
        s = _dot_nt(qg, k_ref[:s_eff, g * HEAD_DIM:(g + 1) * HEAD_DIM])
        s = s.reshape(GROUP, tq, s_eff) + bias[None]
        p = jnp.exp2(s - jnp.max(s, axis=-1, keepdims=True)).astype(BF16)
        v_aug = jnp.concatenate([v_ref[:s_eff, g * HEAD_DIM:(g + 1) * HEAD_DIM], ones], axis=1)
        o = _dot(p.reshape(GROUP * tq, s_eff), v_aug)
        o = o[:, :HEAD_DIM] / o[:, HEAD_DIM:]
        for h in range(GROUP):
            col = (g * GROUP + h) * HEAD_DIM
            o_ref[:, col:col + HEAD_DIM] = o[h * tq:(h + 1) * tq].astype(o_ref.dtype)


def _dsa(qkv, qik, gw, *, batch, seq, ck=512):
    m = qkv.shape[0]
    topk = min(TOPK_MAX, seq // 4)
    kv_w = N_KV_HEADS * HEAD_DIM
    qi_w = N_IDX_HEADS * IDX_DIM
    y = jnp.zeros((m, D_MODEL), BF16)
    for n in range(1, seq // ck + 1):
        tq = 256 if n * ck < seq else 128
        nq = seq // tq
        per = ck // tq
        s_eff, tile0 = n * ck, (n - 1) * per
        row = lambda b, i, tile0=tile0, nq=nq: b * nq + tile0 + i
        y = pl.pallas_call(
            functools.partial(_dsa_kernel, tq=tq, ck=ck, topk=topk, s_eff=s_eff, tile0=tile0),
            grid=(batch, per),
            in_specs=[
                pl.BlockSpec((tq, qi_w), lambda b, i, row=row: (row(b, i), 0)),
                pl.BlockSpec((seq, LANES), lambda b, i: (b, qi_w // LANES)),
                pl.BlockSpec((seq, LANES), lambda b, i: (b, qi_w // LANES + 1)),
                pl.BlockSpec((tq, LANES), lambda b, i, row=row: (row(b, i), 2 * D_MODEL // LANES)),
                pl.BlockSpec((tq, D_MODEL), lambda b, i, row=row: (row(b, i), 0)),
                pl.BlockSpec((seq, kv_w), lambda b, i: (b, D_MODEL // kv_w)),
                pl.BlockSpec((seq, kv_w), lambda b, i: (b, D_MODEL // kv_w + 1)),
                pl.BlockSpec(memory_space=pl.ANY),
            ],
            out_specs=pl.BlockSpec((tq, D_MODEL), lambda b, i, row=row: (row(b, i), 0)),
            out_shape=jax.ShapeDtypeStruct((m, D_MODEL), BF16),
            input_output_aliases={7: 0},
            scratch_shapes=[pltpu.VMEM((tq, s_eff), F32), pltpu.VMEM((tq, s_eff), F32)],
            compiler_params=pltpu.CompilerParams(
                dimension_semantics=("parallel", "arbitrary"),
                vmem_limit_bytes=VMEM_LIMIT_BYTES),
            name=f"dsa_{s_eff}",
        )(qik, qik, qik, gw, qkv, qkv, qkv, y)
    return y


def _merge_ln_kernel(ya_ref, yr_ref, ga_ref, gr_ref, wa_ref, wr_ref, wo_ref, h_ref, g_ref, b_ref, o_ref):
    j = pl.program_id(1)

    @pl.when(j == 0)
    def _():
        o_ref[...] = jnp.zeros_like(o_ref)

    merged = (jax.nn.sigmoid(ga_ref[...]) * _dot(ya_ref[...], wa_ref[...])
              + jax.nn.sigmoid(gr_ref[...]) * _dot(yr_ref[...], wr_ref[...]))
    o_ref[...] += _dot(merged.astype(BF16), wo_ref[...])

    @pl.when(j == pl.num_programs(1) - 1)
    def _():
        y = DN_ALPHA * h_ref[...] + o_ref[...]
        o_ref[...] = _layer_norm(y, g_ref[...], b_ref[...])


def _merge_ln(y_attn, y_rnn, gw, w_attn, w_rnn, w_out, h, g, b, *, tm=512, tn=512):
    m, d = h.shape
    nn = d // tn
    return pl.pallas_call(
        _merge_ln_kernel,
        grid=(m // tm, nn),
        in_specs=[
            pl.BlockSpec((tm, y_attn.shape[1]), lambda i, j: (i, 0)),
            pl.BlockSpec((tm, y_rnn.shape[1]), lambda i, j: (i, 0)),
            pl.BlockSpec((tm, tn), lambda i, j: (i, j)),
            pl.BlockSpec((tm, tn), lambda i, j: (i, j + nn)),
            pl.BlockSpec((w_attn.shape[0], tn), lambda i, j: (0, j)),
            pl.BlockSpec((w_rnn.shape[0], tn), lambda i, j: (0, j)),
            pl.BlockSpec((tn, d), lambda i, j: (j, 0)),
            pl.BlockSpec((tm, d), lambda i, j: (i, 0)),
            pl.BlockSpec((1, d), lambda i, j: (0, 0)),
            pl.BlockSpec((1, d), lambda i, j: (0, 0)),
        ],
        out_specs=pl.BlockSpec((tm, d), lambda i, j: (i, 0)),
        out_shape=jax.ShapeDtypeStruct((m, d), F32),
        compiler_params=pltpu.CompilerParams(
            dimension_semantics=("parallel", "arbitrary"),
            vmem_limit_bytes=VMEM_LIMIT_BYTES),
        name="merge_ln",
    )(y_attn, y_rnn, gw, gw, w_attn, w_rnn, w_out, h, g, b)


def _block_diag_groups(w):
    per = LRU_BLOCKS // LRU_GROUPS
    w4 = w.reshape(LRU_GROUPS, per, LRU_BLOCK, LRU_BLOCK)
    bd = jnp.einsum("gaij,ab->gaibj", w4, jnp.eye(per, dtype=w.dtype))
    return bd.reshape(LRU_GROUPS, LRU_GROUP_W, LRU_GROUP_W)


def kernel(x, positions, ffn1_w_in, ffn1_w_out, ln1_g, ln1_b, w_in, conv_w, conv_b, lru_wa, lru_ba,
           lru_wx, lru_bx, lru_lambda, w_attn_branch, w_rnn_branch, w_out, ln2_g, ln2_b, ffn2_w_in,
           ffn2_w_out, ln3_g, ln3_b):
    batch, seq, d = x.shape
    m = batch * seq
    h = x.reshape(m, d)
    pos_col = positions.reshape(m, 1)
    tab32, tab16 = _rope_tables(pos_col)

    for l in range(DEPTH):
        h, hb = _ffn_ln(h, ffn1_w_in[l].astype(BF16), ffn1_w_out[l].astype(BF16), ln1_g[l][None],
                        ln1_b[l][None], emit_bf16=True)

        wl = w_in[l]
        o_q, o_k, o_v = 0, D_MODEL, D_MODEL + N_KV_HEADS * HEAD_DIM
        o_qi = o_v + N_KV_HEADS * HEAD_DIM
        o_ki = o_qi + N_IDX_HEADS * IDX_DIM
        o_wi = o_ki + IDX_DIM
        o_rx = o_wi + N_IDX_HEADS
        o_gt = o_rx + 2 * D_RNN
        zk = jnp.zeros((d, IDX_DIM), wl.dtype)
        w_ki = wl[:, o_ki:o_wi]
        w_qkv = wl[:, o_q:o_qi].astype(BF16)
        w_qik = jnp.concatenate([wl[:, o_qi:o_ki], w_ki, zk, zk, w_ki], axis=1).astype(BF16)
        w_rxg = wl[:, o_rx:o_gt].astype(BF16)
        pad = jnp.zeros((d, 512 - N_IDX_HEADS), wl.dtype)
        w_gw = jnp.concatenate([wl[:, o_gt:], wl[:, o_wi:o_rx], pad], axis=1).astype(BF16)

        qkv = _proj(hb, w_qkv, BF16, tm=1024, tn=512, tab=tab32, rope_shift=ROPE_DIM // 2,
                    n_rope_tiles=(D_MODEL + N_KV_HEADS * HEAD_DIM) // 512,
                    n_scale_tiles=D_MODEL // 512, scale=HEAD_DIM ** -0.5 * LOG2_E, name="proj_qkv")
        qik = _proj(hb, w_qik, BF16, tm=1024, tn=256, tab=tab16, rope_shift=IDX_ROPE_DIM // 2,
                    n_rope_tiles=w_qik.shape[1] // 256, name="proj_idx")
        rxg = _proj(hb, w_rxg, F32, tm=1024, tn=1024, name="proj_rnn")
        gw = _proj(hb, w_gw, F32, tm=1024, tn=1536, name="proj_gate")

        y_rnn = _rglru(rxg, conv_w[l], conv_b[l][None],
                       _block_diag_groups(lru_wa[l]).astype(BF16), _block_diag_groups(lru_wx[l]).astype(BF16),
                       lru_ba[l][None], lru_bx[l][None], lru_lambda[l][None], batch=batch, seq=seq)
        y_attn = _dsa(qkv, qik, gw, batch=batch, seq=seq)

        h = _merge_ln(y_attn, y_rnn, gw, w_attn_branch[l].astype(BF16), w_rnn_branch[l].astype(BF16),
                      w_out[l].astype(BF16), h, ln2_g[l][None], ln2_b[l][None])
        h = _ffn_ln(h, ffn2_w_in[l].astype(BF16), ffn2_w_out[l].astype(BF16), ln3_g[l][None], ln3_b[l][None],
                    emit_bf16=False)
    return h.reshape(batch, seq, d)
```

```python
import functools

import numpy as np
import jax
import jax.numpy as jnp
from jax import lax
from jax.experimental import pallas as pl
from jax.experimental.pallas import tpu as pltpu

F32 = jnp.float32
BF16 = jnp.bfloat16

D_MODEL = 2048
HEAD_DIM = 128
N_HEADS = D_MODEL // HEAD_DIM
N_KV_HEADS = 4
GROUP = N_HEADS // N_KV_HEADS
ROPE_DIM = HEAD_DIM // 4
ROPE_THETA = 500000.0
N_IDX_HEADS = 16
IDX_DIM = 64
IDX_ROPE_DIM = IDX_DIM // 4
TOPK_MAX = 256
D_RNN = (4 * D_MODEL // 3) // 256 * 256
LRU_BLOCKS = 16
LRU_BLOCK = D_RNN // LRU_BLOCKS
CONV_WIDTH = 4
LRU_C = 8.0
D_FF = (8 * D_MODEL // 3 + 255) // 256 * 256
LN_EPS = 1e-5
DEPTH = 1
DN_ALPHA = (2.0 * DEPTH) ** 0.25

LANES = 128
SUBLANES = 8
MXU_N = 256
VMEM_LIMIT_BYTES = 56 * 1024 * 1024

LRU_GROUPS = 4
LRU_GROUP_W = D_RNN // LRU_GROUPS

INT_MIN = -(2 ** 31)
LOG2_E = 1.4426950408889634


def _layer_norm(y, g, b):
    mu = jnp.mean(y, axis=-1, keepdims=True)
    d = y - mu
    var = jnp.mean(d * d, axis=-1, keepdims=True)
    return d * lax.rsqrt(var + LN_EPS) * g + b


def _dot(a, b):
    return jnp.dot(a, b, preferred_element_type=F32)


def _dot_nt(a, b):
    return lax.dot_general(a, b, (((1,), (1,)), ((), ())), preferred_element_type=F32)


def _ffn_ln_kernel(x_ref, wa_ref, wb_ref, wo_ref, g_ref, b_ref, o_ref, *rest):
    xb_ref = rest[-1]
    j = pl.program_id(1)

    @pl.when(j == 0)
    def _():
        xb_ref[...] = x_ref[...].astype(BF16)
        o_ref[...] = jnp.zeros_like(o_ref)

    xb = xb_ref[...]
    a = _dot(xb, wa_ref[...])
    b = _dot(xb, wb_ref[...])
    act = (jax.nn.silu(a) * b).astype(BF16)
    o_ref[...] += _dot(act, wo_ref[...])

    @pl.when(j == pl.num_programs(1) - 1)
    def _():
        y = DN_ALPHA * x_ref[...] + 0.5 * o_ref[...]
        out = _layer_norm(y, g_ref[...], b_ref[...])
        o_ref[...] = out
        if len(rest) == 2:
            rest[0][...] = out.astype(BF16)


def _ffn_ln(x, w_in, w_out, g, b, *, emit_bf16, tm=512, tf=512):
    m, d = x.shape
    f = w_out.shape[0]
    nf = f // tf
    out_spec = pl.BlockSpec((tm, d), lambda i, j: (i, 0))
    out_specs, out_shape = out_spec, jax.ShapeDtypeStruct((m, d), F32)
    if emit_bf16:
        out_specs, out_shape = [out_spec, out_spec], [out_shape, jax.ShapeDtypeStruct((m, d), BF16)]
    return pl.pallas_call(
        _ffn_ln_kernel,
        grid=(m // tm, nf),
        in_specs=[
            pl.BlockSpec((tm, d), lambda i, j: (i, 0)),
            pl.BlockSpec((d, tf), lambda i, j: (0, j)),
            pl.BlockSpec((d, tf), lambda i, j: (0, j + nf)),
            pl.BlockSpec((tf, d), lambda i, j: (j, 0)),
            pl.BlockSpec((1, d), lambda i, j: (0, 0)),
            pl.BlockSpec((1, d), lambda i, j: (0, 0)),
        ],
        out_specs=out_specs,
        out_shape=out_shape,
        scratch_shapes=[pltpu.VMEM((tm, d), BF16)],
        compiler_params=pltpu.CompilerParams(
            dimension_semantics=("parallel", "arbitrary"),
            vmem_limit_bytes=VMEM_LIMIT_BYTES),
        name="ffn_ln",
    )(x, w_in, w_in, w_out, g, b)


def _rope_tab_kernel(pos_ref, f_ref, o32_ref, o16_ref):
    pos = pos_ref[...].astype(F32)
    for row, o_ref in ((0, o32_ref), (3, o16_ref)):
        ang = pos * f_ref[row:row + 1, :]
        s = jnp.sin(ang)
        o_ref[0] = jnp.cos(ang)
        o_ref[1] = s * f_ref[row + 1:row + 2, :]
        o_ref[2] = s * f_ref[row + 2:row + 3, :]


def _rope_lane_table():
    lane = np.arange(LANES)

    def rows(rot_dim, period):
        half = rot_dim // 2
        inv_freq = jnp.power(ROPE_THETA, -(jnp.arange(half, dtype=F32) * 2.0 / rot_dim))
        l = lane % period
        freq = jnp.where(l < rot_dim, inv_freq[l % half], 0.0)
        plus = ((l >= half) & (l < rot_dim)).astype(np.float32)
        minus = -(l < half).astype(np.float32)
        return [freq, jnp.asarray(plus), jnp.asarray(minus)]

    z = jnp.zeros((LANES,), F32)
    return jnp.stack(rows(ROPE_DIM, HEAD_DIM) + rows(IDX_ROPE_DIM, IDX_DIM) + [z, z]).astype(F32)


def _rope_tables(pos_col, *, tm=1024):
    m = pos_col.shape[0]
    shp = jax.ShapeDtypeStruct((3, m, LANES), F32)
    return pl.pallas_call(
        _rope_tab_kernel,
        grid=(m // tm,),
        in_specs=[pl.BlockSpec((tm, 1), lambda i: (i, 0)),
                  pl.BlockSpec((8, LANES), lambda i: (0, 0))],
        out_specs=[pl.BlockSpec((3, tm, LANES), lambda i: (0, i, 0)),
                   pl.BlockSpec((3, tm, LANES), lambda i: (0, i, 0))],
        out_shape=[shp, shp],
        compiler_params=pltpu.CompilerParams(dimension_semantics=("parallel",)),
        name="rope_tables",
    )(pos_col, _rope_lane_table())


def _proj_kernel(x_ref, w_ref, *rest, tn, rope_shift, n_rope_tiles, n_scale_tiles, scale):
    if rope_shift is None:
        (o_ref,) = rest
        o_ref[...] = _dot(x_ref[...], w_ref[...]).astype(o_ref.dtype)
        return
    tab_ref, o_ref = rest
    j = pl.program_id(1)
    roped = j < n_rope_tiles
    sc = jnp.where(j < n_scale_tiles, scale, 1.0).astype(F32)
    c = jnp.where(roped, tab_ref[0], 1.0) * sc
    s_hi = jnp.where(roped, tab_ref[1], 0.0) * sc
    s_lo = jnp.where(roped, tab_ref[2], 0.0) * sc
    x = x_ref[...]
    for u in range(tn // MXU_N):
        acc = _dot(x, w_ref[:, u * MXU_N:(u + 1) * MXU_N])
        for t in range(MXU_N // LANES):
            a = acc[:, t * LANES:(t + 1) * LANES]
            r = (a * c + pltpu.roll(a, rope_shift, 1) * s_hi
                 + pltpu.roll(a, LANES - rope_shift, 1) * s_lo)
            col = u * MXU_N + t * LANES
            o_ref[:, col:col + LANES] = r.astype(o_ref.dtype)


def _proj(xb, w, out_dtype, *, tm, tn, tab=None, rope_shift=None, n_rope_tiles=0,
          n_scale_tiles=0, scale=1.0, name="proj"):
    m, d = xb.shape
    n = w.shape[1]
    in_specs = [pl.BlockSpec((tm, d), lambda i, j: (i, 0)),
                pl.BlockSpec((d, tn), lambda i, j: (0, j))]
    args = [xb, w]
    if rope_shift is not None:
        in_specs.append(pl.BlockSpec((3, tm, LANES), lambda i, j: (0, i, 0)))
        args.append(tab)
    return pl.pallas_call(
        functools.partial(_proj_kernel, tn=tn, rope_shift=rope_shift, n_rope_tiles=n_rope_tiles,
                          n_scale_tiles=n_scale_tiles, scale=scale),
        grid=(m // tm, n // tn),
        in_specs=in_specs,
        out_specs=pl.BlockSpec((tm, tn), lambda i, j: (i, j)),
        out_shape=jax.ShapeDtypeStruct((m, n), out_dtype),
        compiler_params=pltpu.CompilerParams(
            dimension_semantics=("parallel", "arbitrary"),
            vmem_limit_bytes=VMEM_LIMIT_BYTES),
        name=name,
    )(*args)


def _rglru_kernel(rx_ref, rg_ref, cw_ref, cb_ref, wa_ref, wx_ref, ba_ref, bx_ref, lam_ref,
                  o_ref, xbuf, a_s, b_s, h_s, *, tt):
    t = pl.program_id(2)
    w = xbuf.shape[1]

    @pl.when(t == 0)
    def _():
        xbuf[0:SUBLANES, :] = jnp.zeros((SUBLANES, w), F32)
        h_s[...] = jnp.zeros_like(h_s)

    xbuf[SUBLANES:SUBLANES + tt, :] = rx_ref[...]
    xc = cb_ref[...]
    for k in range(CONV_WIDTH):
        off = SUBLANES - (CONV_WIDTH - 1) + k
        xc = xc + cw_ref[k:k + 1, :] * xbuf[off:off + tt, :]
    xbuf[0:SUBLANES, :] = rx_ref[tt - SUBLANES:tt, :]

    xcb = xc.astype(BF16)
    r = jax.nn.sigmoid(_dot(xcb, wa_ref[0]) + ba_ref[...])
    ig = jax.nn.sigmoid(_dot(xcb, wx_ref[0]) + bx_ref[...])
    log_a = -LRU_C * r * jax.nn.softplus(-lam_ref[...])
    a = jnp.exp(log_a)
    a_s[...] = a
    b_s[...] = jnp.sqrt(-jnp.tanh(log_a) * (a * a + 1.0)) * (ig * xc)

    row = lax.broadcasted_iota(jnp.int32, (SUBLANES, w), 0)

    def body(g, h):
        sl = pl.ds(pl.multiple_of(g * SUBLANES, SUBLANES), SUBLANES)
        a8 = a_s[sl, :]
        b8 = b_s[sl, :]
        for d in (1, 2, 4):
            keep = row >= d
            a_sh = jnp.where(keep, pltpu.roll(a8, d, 0), 1.0)
            b_sh = jnp.where(keep, pltpu.roll(b8, d, 0), 0.0)
            b8 = a8 * b_sh + b8
            a8 = a8 * a_sh
        hrows = a8 * h + b8
        b_s[sl, :] = hrows
        return jnp.broadcast_to(hrows[SUBLANES - 1:SUBLANES, :], (SUBLANES, w))

    h_s[...] = lax.fori_loop(0, tt // SUBLANES, body, h_s[...])
    o_ref[...] = (b_s[...] * jax.nn.gelu(rg_ref[...])).astype(o_ref.dtype)


def _rglru(rxg, conv_w, conv_b, wa_bd, wx_bd, ba, bx, lam, *, batch, seq, tt=512):
    m = rxg.shape[0]
    nt = seq // tt
    gw = LRU_GROUP_W
    vec = lambda: pl.BlockSpec((1, gw), lambda b, g, t: (0, g))
    return pl.pallas_call(
        functools.partial(_rglru_kernel, tt=tt),
        grid=(batch, LRU_GROUPS, nt),
        in_specs=[
            pl.BlockSpec((tt, gw), lambda b, g, t: (b * nt + t, g)),
            pl.BlockSpec((tt, gw), lambda b, g, t: (b * nt + t, g + LRU_GROUPS)),
            pl.BlockSpec((CONV_WIDTH, gw), lambda b, g, t: (0, g)),
            vec(),
            pl.BlockSpec((1, gw, gw), lambda b, g, t: (g, 0, 0)),
            pl.BlockSpec((1, gw, gw), lambda b, g, t: (g, 0, 0)),
            vec(), vec(), vec(),
        ],
        out_specs=pl.BlockSpec((tt, gw), lambda b, g, t: (b * nt + t, g)),
        out_shape=jax.ShapeDtypeStruct((m, D_RNN), BF16),
        scratch_shapes=[pltpu.VMEM((tt + SUBLANES, gw), F32), pltpu.VMEM((tt, gw), F32),
                        pltpu.VMEM((tt, gw), F32), pltpu.VMEM((SUBLANES, gw), F32)],
        compiler_params=pltpu.CompilerParams(
            dimension_semantics=("parallel", "parallel", "arbitrary"),
            vmem_limit_bytes=VMEM_LIMIT_BYTES),
        name="rglru",
    )(rxg, rxg, conv_w, conv_b, wa_bd, wx_bd, ba, bx, lam)


def _dsa_kernel(qi_ref, kie_ref, kio_ref, wi_ref, q_ref, k_ref, v_ref, prev_ref, o_ref,
                score_ref, bias_ref, *, tq, ck, topk, s_eff, tile0):
    del prev_ref
    i = tile0 + pl.program_id(1)
    n_pair = N_IDX_HEADS // 2

    qi = qi_ref[...]
    qs = jnp.concatenate([qi[:, j * LANES:(j + 1) * LANES] for j in range(n_pair)], axis=0)
    w = wi_ref[...] * (N_IDX_HEADS ** -0.5 * IDX_DIM ** -0.5)
    for c in range(s_eff // ck):
        le = _dot_nt(qs, kie_ref[c * ck:(c + 1) * ck, :])
        lo = _dot_nt(qs, kio_ref[c * ck:(c + 1) * ck, :])
        acc = jnp.zeros((tq, ck), F32)
        for j in range(n_pair):
            acc = acc + jnp.maximum(le[j * tq:(j + 1) * tq], 0.0) * w[:, 2 * j:2 * j + 1]
            acc = acc + jnp.maximum(lo[j * tq:(j + 1) * tq], 0.0) * w[:, 2 * j + 1:2 * j + 2]
        score_ref[:, c * ck:(c + 1) * ck] = acc

    qpos = i * tq + lax.broadcasted_iota(jnp.int32, (tq, s_eff), 0)
    kpos = lax.broadcasted_iota(jnp.int32, (tq, s_eff), 1)
    score = jnp.where(kpos <= qpos, score_ref[...], -jnp.inf)

    def key_to_float(key):
        return pltpu.bitcast(jnp.where(key < 0, key ^ 0x7FFFFFFF, key), F32)

    def search(it, key):
        cand = key ^ lax.shift_left(jnp.int32(1), 31 - it)
        cnt = jnp.sum(jnp.where(score >= key_to_float(cand), 1.0, 0.0), axis=1, keepdims=True)
        return jnp.where(cnt >= topk, cand, key)

    key = lax.fori_loop(0, 32, search, jnp.full((tq, 1), INT_MIN, jnp.int32))
    thr = key_to_float(jnp.maximum(key, INT_MIN + 0x00800000))
    ge = score >= thr
    cnt = jnp.sum(jnp.where(ge, 1.0, 0.0), axis=1, keepdims=True)
    bias_ref[...] = jnp.where(ge, 0.0, -jnp.inf).astype(F32)

    @pl.when(jnp.max(cnt) > topk)
    def _():
        tw = 2 * LANES
        need = topk - jnp.sum(jnp.where(score > thr, 1.0, 0.0), axis=1, keepdims=True)
        tri = jnp.where(lax.broadcasted_iota(jnp.int32, (tw, tw), 0)
                        <= lax.broadcasted_iota(jnp.int32, (tw, tw), 1), 1.0, 0.0).astype(BF16)
        carry = jnp.zeros((tq, 1), F32)
        for c in range(s_eff // tw):
            kc = score[:, c * tw:(c + 1) * tw]
            eqc = jnp.where(kc == thr, 1.0, 0.0)
            rank = _dot(eqc.astype(BF16), tri) + carry
            take = (kc > thr) | ((kc == thr) & (rank <= need))
            bias_ref[:, c * tw:(c + 1) * tw] = jnp.where(take, 0.0, -jnp.inf).astype(F32)
            carry = carry + jnp.sum(eqc, axis=1, keepdims=True)

    bias = bias_ref[...]
    q = q_ref[...]
    ones = jnp.ones((s_eff, HEAD_DIM), BF16)
    for g in range(N_KV_HEADS):
        qg = jnp.concatenate(
            [q[:, (g * GROUP + h) * HEAD_DIM:(g * GROUP + h + 1) * HEAD_DIM] for h in range(GROUP)],
            axis=0)
        s = _dot_nt(qg, k_ref[:s_eff, g * HEAD_DIM:(g + 1) * HEAD_DIM])
        s = s.reshape(GROUP, tq, s_eff) + bias[None]
        p = jnp.exp2(s - jnp.max(s, axis=-1, keepdims=True)).astype(BF16)
        v_aug = jnp.concatenate([v_ref[:s_eff, g * HEAD_DIM:(g + 1) * HEAD_DIM], ones], axis=1)
        o = _dot(p.reshape(GROUP * tq, s_eff), v_aug)
        o = o[:, :HEAD_DIM] / o[:, HEAD_DIM:]
        for h in range(GROUP):
            col = (g * GROUP + h) * HEAD_DIM
            o_ref[:, col:col + HEAD_DIM] = o[h * tq:(h + 1) * tq].astype(o_ref.dtype)


def _dsa(qkv, qik, gw, *, batch, seq, tq=256, ck=512):
    m = qkv.shape[0]
    topk = min(TOPK_MAX, seq // 4)
    kv_w = N_KV_HEADS * HEAD_DIM
    qi_w = N_IDX_HEADS * IDX_DIM
    y = jnp.zeros((m, D_MODEL), BF16)
    for n in range(1, seq // ck + 1):
        nq = seq // tq
        per = ck // tq
        s_eff, tile0 = n * ck, (n - 1) * per
        row = lambda b, i, tile0=tile0, nq=nq: b * nq + tile0 + i
        y = pl.pallas_call(
            functools.partial(_dsa_kernel, tq=tq, ck=ck, topk=topk, s_eff=s_eff, tile0=tile0),
            grid=(batch, per),
            in_specs=[
                pl.BlockSpec((tq, qi_w), lambda b, i, row=row: (row(b, i), 0)),
                pl.BlockSpec((seq, LANES), lambda b, i: (b, qi_w // LANES)),
                pl.BlockSpec((seq, LANES), lambda b, i: (b, qi_w // LANES + 1)),
                pl.BlockSpec((tq, LANES), lambda b, i, row=row: (row(b, i), 2 * D_MODEL // LANES)),
                pl.BlockSpec((tq, D_MODEL), lambda b, i, row=row: (row(b, i), 0)),
                pl.BlockSpec((seq, kv_w), lambda b, i: (b, D_MODEL // kv_w)),
                pl.BlockSpec((seq, kv_w), lambda b, i: (b, D_MODEL // kv_w + 1)),
                pl.BlockSpec(memory_space=pl.ANY),
            ],
            out_specs=pl.BlockSpec((tq, D_MODEL), lambda b, i, row=row: (row(b, i), 0)),
            out_shape=jax.ShapeDtypeStruct((m, D_MODEL), BF16),
            input_output_aliases={7: 0},
            scratch_shapes=[pltpu.VMEM((tq, s_eff), F32), pltpu.VMEM((tq, s_eff), F32)],
            compiler_params=pltpu.CompilerParams(
                dimension_semantics=("parallel", "arbitrary"),
                vmem_limit_bytes=VMEM_LIMIT_BYTES),
            name=f"dsa_{s_eff}",
        )(qik, qik, qik, gw, qkv, qkv, qkv, y)
    return y


def _merge_ln_kernel(ya_ref, yr_ref, ga_ref, gr_ref, wa_ref, wr_ref, wo_ref, h_ref, g_ref, b_ref, o_ref):
    j = pl.program_id(1)

    @pl.when(j == 0)
    def _():
        o_ref[...] = jnp.zeros_like(o_ref)

    merged = (jax.nn.sigmoid(ga_ref[...]) * _dot(ya_ref[...], wa_ref[...])
              + jax.nn.sigmoid(gr_ref[...]) * _dot(yr_ref[...], wr_ref[...]))
    o_ref[...] += _dot(merged.astype(BF16), wo_ref[...])

    @pl.when(j == pl.num_programs(1) - 1)
    def _():
        y = DN_ALPHA * h_ref[...] + o_ref[...]
        o_ref[...] = _layer_norm(y, g_ref[...], b_ref[...])


def _merge_ln(y_attn, y_rnn, gw, w_attn, w_rnn, w_out, h, g, b, *, tm=512, tn=512):
    m, d = h.shape
    nn = d // tn
    return pl.pallas_call(
        _merge_ln_kernel,
        grid=(m // tm, nn),
        in_specs=[
            pl.BlockSpec((tm, y_attn.shape[1]), lambda i, j: (i, 0)),
            pl.BlockSpec((tm, y_rnn.shape[1]), lambda i, j: (i, 0)),
            pl.BlockSpec((tm, tn), lambda i, j: (i, j)),
            pl.BlockSpec((tm, tn), lambda i, j: (i, j + nn)),
            pl.BlockSpec((w_attn.shape[0], tn), lambda i, j: (0, j)),
            pl.BlockSpec((w_rnn.shape[0], tn), lambda i, j: (0, j)),
            pl.BlockSpec((tn, d), lambda i, j: (j, 0)),
            pl.BlockSpec((tm, d), lambda i, j: (i, 0)),
            pl.BlockSpec((1, d), lambda i, j: (0, 0)),
            pl.BlockSpec((1, d), lambda i, j: (0, 0)),
        ],
        out_specs=pl.BlockSpec((tm, d), lambda i, j: (i, 0)),
        out_shape=jax.ShapeDtypeStruct((m, d), F32),
        compiler_params=pltpu.CompilerParams(
            dimension_semantics=("parallel", "arbitrary"),
            vmem_limit_bytes=VMEM_LIMIT_BYTES),
        name="merge_ln",
    )(y_attn, y_rnn, gw, gw, w_attn, w_rnn, w_out, h, g, b)


def _block_diag_groups(w):
    per = LRU_BLOCKS // LRU_GROUPS
    w4 = w.reshape(LRU_GROUPS, per, LRU_BLOCK, LRU_BLOCK)
    bd = jnp.einsum("gaij,ab->gaibj", w4, jnp.eye(per, dtype=w.dtype))
    return bd.reshape(LRU_GROUPS, LRU_GROUP_W, LRU_GROUP_W)


def kernel(x, positions, ffn1_w_in, ffn1_w_out, ln1_g, ln1_b, w_in, conv_w, conv_b, lru_wa, lru_ba,
           lru_wx, lru_bx, lru_lambda, w_attn_branch, w_rnn_branch, w_out, ln2_g, ln2_b, ffn2_w_in,
           ffn2_w_out, ln3_g, ln3_b):
    batch, seq, d = x.shape
    m = batch * seq
    h = x.reshape(m, d)
    pos_col = positions.reshape(m, 1)
    tab32, tab16 = _rope_tables(pos_col)

    for l in range(DEPTH):
        h, hb = _ffn_ln(h, ffn1_w_in[l].astype(BF16), ffn1_w_out[l].astype(BF16), ln1_g[l][None],
                        ln1_b[l][None], emit_bf16=True)

        wl = w_in[l]
        o_q, o_k, o_v = 0, D_MODEL, D_MODEL + N_KV_HEADS * HEAD_DIM
        o_qi = o_v + N_KV_HEADS * HEAD_DIM
        o_ki = o_qi + N_IDX_HEADS * IDX_DIM
        o_wi = o_ki + IDX_DIM
        o_rx = o_wi + N_IDX_HEADS
        o_gt = o_rx + 2 * D_RNN
        zk = jnp.zeros((d, IDX_DIM), wl.dtype)
        w_ki = wl[:, o_ki:o_wi]
        w_qkv = wl[:, o_q:o_qi].astype(BF16)
        w_qik = jnp.concatenate([wl[:, o_qi:o_ki], w_ki, zk, zk, w_ki], axis=1).astype(BF16)
        w_rxg = wl[:, o_rx:o_gt].astype(BF16)
        pad = jnp.zeros((d, 512 - N_IDX_HEADS), wl.dtype)
        w_gw = jnp.concatenate([wl[:, o_gt:], wl[:, o_wi:o_rx], pad], axis=1).astype(BF16)

        qkv = _proj(hb, w_qkv, BF16, tm=1024, tn=512, tab=tab32, rope_shift=ROPE_DIM // 2,
                    n_rope_tiles=(D_MODEL + N_KV_HEADS * HEAD_DIM) // 512,
                    n_scale_tiles=D_MODEL // 512, scale=HEAD_DIM ** -0.5 * LOG2_E, name="proj_qkv")
        qik = _proj(hb, w_qik, BF16, tm=1024, tn=256, tab=tab16, rope_shift=IDX_ROPE_DIM // 2,
                    n_rope_tiles=w_qik.shape[1] // 256, name="proj_idx")
        rxg = _proj(hb, w_rxg, F32, tm=1024, tn=1024, name="proj_rnn")
        gw = _proj(hb, w_gw, F32, tm=1024, tn=1536, name="proj_gate")

        y_rnn = _rglru(rxg, conv_w[l], conv_b[l][None],
                       _block_diag_groups(lru_wa[l]).astype(BF16), _block_diag_groups(lru_wx[l]).astype(BF16),
                       lru_ba[l][None], lru_bx[l][None], lru_lambda[l][None], batch=batch, seq=seq)
        y_attn = _dsa(qkv, qik, gw, batch=batch, seq=seq)

        h = _merge_ln(y_attn, y_rnn, gw, w_attn_branch[l].astype(BF16), w_rnn_branch[l].astype(BF16),
                      w_out[l].astype(BF16), h, ln2_g[l][None], ln2_b[l][None])
        h = _ffn_ln(h, ffn2_w_in[l].astype(BF16), ffn2_w_out[l].astype(BF16), ln3_g[l][None], ln3_b[l][None],
                    emit_bf16=False)
    return h.reshape(batch, seq, d)
```

```python
import functools

import numpy as np
import jax
import jax.numpy as jnp
from jax import lax
from jax.experimental import pallas as pl
from jax.experimental.pallas import tpu as pltpu

F32 = jnp.float32
BF16 = jnp.bfloat16

D_MODEL = 2048
HEAD_DIM = 128
N_HEADS = D_MODEL // HEAD_DIM
N_KV_HEADS = 4
GROUP = N_HEADS // N_KV_HEADS
ROPE_DIM = HEAD_DIM // 4
ROPE_THETA = 500000.0
N_IDX_HEADS = 16
IDX_DIM = 64
IDX_ROPE_DIM = IDX_DIM // 4
TOPK_MAX = 256
D_RNN = (4 * D_MODEL // 3) // 256 * 256
LRU_BLOCKS = 16
LRU_BLOCK = D_RNN // LRU_BLOCKS
CONV_WIDTH = 4
LRU_C = 8.0
D_FF = (8 * D_MODEL // 3 + 255) // 256 * 256
LN_EPS = 1e-5
DEPTH = 1
DN_ALPHA = (2.0 * DEPTH) ** 0.25

LANES = 128
SUBLANES = 8
MXU_N = 256
VMEM_LIMIT_BYTES = 56 * 1024 * 1024

LRU_GROUPS = 4
LRU_GROUP_W = D_RNN // LRU_GROUPS

INT_MIN = -(2 ** 31)
LOG2_E = 1.4426950408889634


def _layer_norm(y, g, b):
    mu = jnp.mean(y, axis=-1, keepdims=True)
    d = y - mu
    var = jnp.mean(d * d, axis=-1, keepdims=True)
    return d * lax.rsqrt(var + LN_EPS) * g + b


def _dot(a, b):
    return jnp.dot(a, b, preferred_element_type=F32)


def _dot_nt(a, b):
    return lax.dot_general(a, b, (((1,), (1,)), ((), ())), preferred_element_type=F32)


def _ffn_ln_kernel(x_ref, wa_ref, wb_ref, wo_ref, g_ref, b_ref, o_ref, *rest):
    xb_ref = rest[-1]
    j = pl.program_id(1)

    @pl.when(j == 0)
    def _():
        xb_ref[...] = x_ref[...].astype(BF16)
        o_ref[...] = jnp.zeros_like(o_ref)

    xb = xb_ref[...]
    a = _dot(xb, wa_ref[...])
    b = _dot(xb, wb_ref[...])
    act = (jax.nn.silu(a) * b).astype(BF16)
    o_ref[...] += _dot(act, wo_ref[...])

    @pl.when(j == pl.num_programs(1) - 1)
    def _():
        y = DN_ALPHA * x_ref[...] + 0.5 * o_ref[...]
        out = _layer_norm(y, g_ref[...], b_ref[...])
        o_ref[...] = out
        if len(rest) == 2:
            rest[0][...] = out.astype(BF16)


def _ffn_ln(x, w_in, w_out, g, b, *, emit_bf16, tm=512, tf=512):
    m, d = x.shape
    f = w_out.shape[0]
    nf = f // tf
    out_spec = pl.BlockSpec((tm, d), lambda i, j: (i, 0))
    out_specs, out_shape = out_spec, jax.ShapeDtypeStruct((m, d), F32)
    if emit_bf16:
        out_specs, out_shape = [out_spec, out_spec], [out_shape, jax.ShapeDtypeStruct((m, d), BF16)]
    return pl.pallas_call(
        _ffn_ln_kernel,
        grid=(m // tm, nf),
        in_specs=[
            pl.BlockSpec((tm, d), lambda i, j: (i, 0)),
            pl.BlockSpec((d, tf), lambda i, j: (0, j)),
            pl.BlockSpec((d, tf), lambda i, j: (0, j + nf)),
            pl.BlockSpec((tf, d), lambda i, j: (j, 0)),
            pl.BlockSpec((1, d), lambda i, j: (0, 0)),
            pl.BlockSpec((1, d), lambda i, j: (0, 0)),
        ],
        out_specs=out_specs,
        out_shape=out_shape,
        scratch_shapes=[pltpu.VMEM((tm, d), BF16)],
        compiler_params=pltpu.CompilerParams(
            dimension_semantics=("parallel", "arbitrary"),
            vmem_limit_bytes=VMEM_LIMIT_BYTES),
        name="ffn_ln",
    )(x, w_in, w_in, w_out, g, b)


HALF_TILE = LANES // 2
PAIR_LANES = ROPE_DIM // 2


def _rope_tab_kernel(pos_ref, f_ref, o32_ref, o16_ref):
    pos = pos_ref[...].astype(F32)
    for row, o_ref in ((0, o32_ref), (1, o16_ref)):
        ang = pos * f_ref[row:row + 1, :]
        o_ref[0] = jnp.cos(ang)
        o_ref[1] = jnp.sin(ang) * f_ref[2:3, :]


def _rope_lane_table():
    lane = np.arange(LANES)
    first = lane < PAIR_LANES
    second = (lane >= HALF_TILE) & (lane < HALF_TILE + PAIR_LANES)

    def freq(rot_dim):
        half = rot_dim // 2
        inv_freq = jnp.power(ROPE_THETA, -(jnp.arange(half, dtype=F32) * 2.0 / rot_dim))
        return jnp.where(first | second, inv_freq[lane % half], 0.0)

    sign = jnp.asarray(second.astype(np.float32) - first.astype(np.float32))
    z = jnp.zeros((LANES,), F32)
    return jnp.stack([freq(ROPE_DIM), freq(IDX_ROPE_DIM), sign, z, z, z, z, z]).astype(F32)


def _pair_layout(w, idx_pairs):
    d, n = w.shape
    t = w.reshape(d, n // LANES, LANES)
    if idx_pairs:
        q8, hd = IDX_ROPE_DIM // 2, IDX_DIM
        parts = [t[..., 0:q8], t[..., hd:hd + q8], t[..., 2 * q8:hd],
                 t[..., q8:2 * q8], t[..., hd + q8:hd + 2 * q8], t[..., hd + 2 * q8:]]
    else:
        h = ROPE_DIM // 2
        parts = [t[..., 0:h], t[..., 2 * h:HALF_TILE + h], t[..., h:2 * h], t[..., HALF_TILE + h:]]
    return jnp.concatenate(parts, axis=-1).reshape(d, n)


def _rope_tables(pos_col, *, tm=1024):
    m = pos_col.shape[0]
    shp = jax.ShapeDtypeStruct((2, m, LANES), F32)
    return pl.pallas_call(
        _rope_tab_kernel,
        grid=(m // tm,),
        in_specs=[pl.BlockSpec((tm, 1), lambda i: (i, 0)),
                  pl.BlockSpec((8, LANES), lambda i: (0, 0))],
        out_specs=[pl.BlockSpec((2, tm, LANES), lambda i: (0, i, 0)),
                   pl.BlockSpec((2, tm, LANES), lambda i: (0, i, 0))],
        out_shape=[shp, shp],
        compiler_params=pltpu.CompilerParams(dimension_semantics=("parallel",)),
        name="rope_tables",
    )(pos_col, _rope_lane_table())


def _proj_kernel(x_ref, w_ref, *rest, tn, rope_shift, n_rope_tiles, n_scale_tiles, scale):
    if rope_shift is None:
        (o_ref,) = rest
        o_ref[...] = _dot(x_ref[...], w_ref[...]).astype(o_ref.dtype)
        return
    tab_ref, o_ref = rest
    j = pl.program_id(1)
    roped = j < n_rope_tiles
    sc = jnp.where(j < n_scale_tiles, scale, 1.0).astype(F32)
    c = jnp.where(roped, tab_ref[0], 1.0) * sc
    s = jnp.where(roped, tab_ref[1], 0.0) * sc
    x = x_ref[...]
    for u in range(tn // MXU_N):
        acc = _dot(x, w_ref[:, u * MXU_N:(u + 1) * MXU_N])
        for t in range(MXU_N // LANES):
            a = acc[:, t * LANES:(t + 1) * LANES]
            r = a * c + pltpu.roll(a, rope_shift, 1) * s
            col = u * MXU_N + t * LANES
            o_ref[:, col:col + LANES] = r.astype(o_ref.dtype)


def _proj(xb, w, out_dtype, *, tm, tn, tab=None, rope_shift=None, n_rope_tiles=0,
          n_scale_tiles=0, scale=1.0, name="proj"):
    m, d = xb.shape
    n = w.shape[1]
    in_specs = [pl.BlockSpec((tm, d), lambda i, j: (i, 0)),
                pl.BlockSpec((d, tn), lambda i, j: (0, j))]
    args = [xb, w]
    if rope_shift is not None:
        in_specs.append(pl.BlockSpec((2, tm, LANES), lambda i, j: (0, i, 0)))
        args.append(tab)
    return pl.pallas_call(
        functools.partial(_proj_kernel, tn=tn, rope_shift=rope_shift, n_rope_tiles=n_rope_tiles,
                          n_scale_tiles=n_scale_tiles, scale=scale),
        grid=(m // tm, n // tn),
        in_specs=in_specs,
        out_specs=pl.BlockSpec((tm, tn), lambda i, j: (i, j)),
        out_shape=jax.ShapeDtypeStruct((m, n), out_dtype),
        compiler_params=pltpu.CompilerParams(
            dimension_semantics=("parallel", "arbitrary"),
            vmem_limit_bytes=VMEM_LIMIT_BYTES),
        name=name,
    )(*args)


def _rglru_kernel(rx_ref, rg_ref, cw_ref, cb_ref, wa_ref, wx_ref, ba_ref, bx_ref, lam_ref,
                  o_ref, xbuf, a_s, b_s, h_s, *, tt):
    t = pl.program_id(2)
    w = xbuf.shape[1]

    @pl.when(t == 0)
    def _():
        xbuf[0:SUBLANES, :] = jnp.zeros((SUBLANES, w), F32)
        h_s[...] = jnp.zeros_like(h_s)

    xbuf[SUBLANES:SUBLANES + tt, :] = rx_ref[...]
    xc = cb_ref[...]
    for k in range(CONV_WIDTH):
        off = SUBLANES - (CONV_WIDTH - 1) + k
        xc = xc + cw_ref[k:k + 1, :] * xbuf[off:off + tt, :]
    xbuf[0:SUBLANES, :] = rx_ref[tt - SUBLANES:tt, :]

    xcb = xc.astype(BF16)
    r = jax.nn.sigmoid(_dot(xcb, wa_ref[0]) + ba_ref[...])
    ig = jax.nn.sigmoid(_dot(xcb, wx_ref[0]) + bx_ref[...])
    log_a = -LRU_C * r * jax.nn.softplus(-lam_ref[...])
    a = jnp.exp(log_a)
    a_s[...] = a
    b_s[...] = jnp.sqrt(-jnp.tanh(log_a) * (a * a + 1.0)) * (ig * xc)

    row = lax.broadcasted_iota(jnp.int32, (SUBLANES, w), 0)

    def body(g, h):
        sl = pl.ds(pl.multiple_of(g * SUBLANES, SUBLANES), SUBLANES)
        a8 = a_s[sl, :]
        b8 = b_s[sl, :]
        for d in (1, 2, 4):
            keep = row >= d
            a_sh = jnp.where(keep, pltpu.roll(a8, d, 0), 1.0)
            b_sh = jnp.where(keep, pltpu.roll(b8, d, 0), 0.0)
            b8 = a8 * b_sh + b8
            a8 = a8 * a_sh
        hrows = a8 * h + b8
        b_s[sl, :] = hrows
        return jnp.broadcast_to(hrows[SUBLANES - 1:SUBLANES, :], (SUBLANES, w))

    h_s[...] = lax.fori_loop(0, tt // SUBLANES, body, h_s[...])
    o_ref[...] = (b_s[...] * jax.nn.gelu(rg_ref[...])).astype(o_ref.dtype)


def _rglru(rxg, conv_w, conv_b, wa_bd, wx_bd, ba, bx, lam, *, batch, seq, tt=512):
    m = rxg.shape[0]
    nt = seq // tt
    gw = LRU_GROUP_W
    vec = lambda: pl.BlockSpec((1, gw), lambda b, g, t: (0, g))
    return pl.pallas_call(
        functools.partial(_rglru_kernel, tt=tt),
        grid=(batch, LRU_GROUPS, nt),
        in_specs=[
            pl.BlockSpec((tt, gw), lambda b, g, t: (b * nt + t, g)),
            pl.BlockSpec((tt, gw), lambda b, g, t: (b * nt + t, g + LRU_GROUPS)),
            pl.BlockSpec((CONV_WIDTH, gw), lambda b, g, t: (0, g)),
            vec(),
            pl.BlockSpec((1, gw, gw), lambda b, g, t: (g, 0, 0)),
            pl.BlockSpec((1, gw, gw), lambda b, g, t: (g, 0, 0)),
            vec(), vec(), vec(),
        ],
        out_specs=pl.BlockSpec((tt, gw), lambda b, g, t: (b * nt + t, g)),
        out_shape=jax.ShapeDtypeStruct((m, D_RNN), BF16),
        scratch_shapes=[pltpu.VMEM((tt + SUBLANES, gw), F32), pltpu.VMEM((tt, gw), F32),
                        pltpu.VMEM((tt, gw), F32), pltpu.VMEM((SUBLANES, gw), F32)],
        compiler_params=pltpu.CompilerParams(
            dimension_semantics=("parallel", "parallel", "arbitrary"),
            vmem_limit_bytes=VMEM_LIMIT_BYTES),
        name="rglru",
    )(rxg, rxg, conv_w, conv_b, wa_bd, wx_bd, ba, bx, lam)


def _dsa_kernel(qi_ref, kie_ref, kio_ref, wi_ref, q_ref, k_ref, v_ref, prev_ref, o_ref,
                score_ref, bias_ref, *, tq, ck, topk, s_eff, tile0):
    del prev_ref
    i = tile0 + pl.program_id(1)
    n_pair = N_IDX_HEADS // 2

    qi = qi_ref[...]
    qs = jnp.concatenate([qi[:, j * LANES:(j + 1) * LANES] for j in range(n_pair)], axis=0)
    w = wi_ref[...] * (N_IDX_HEADS ** -0.5 * IDX_DIM ** -0.5)
    for c in range(s_eff // ck):
        le = _dot_nt(qs, kie_ref[c * ck:(c + 1) * ck, :])
        lo = _dot_nt(qs, kio_ref[c * ck:(c + 1) * ck, :])
        acc = jnp.zeros((tq, ck), F32)
        for j in range(n_pair):
            acc = acc + jnp.maximum(le[j * tq:(j + 1) * tq], 0.0) * w[:, 2 * j:2 * j + 1]
            acc = acc + jnp.maximum(lo[j * tq:(j + 1) * tq], 0.0) * w[:, 2 * j + 1:2 * j + 2]
        score_ref[:, c * ck:(c + 1) * ck] = acc

    qpos = i * tq + lax.broadcasted_iota(jnp.int32, (tq, s_eff), 0)
    kpos = lax.broadcasted_iota(jnp.int32, (tq, s_eff), 1)
    score = jnp.where(kpos <= qpos, score_ref[...], -jnp.inf)

    def key_to_float(key):
        return pltpu.bitcast(jnp.where(key < 0, key ^ 0x7FFFFFFF, key), F32)

    def search(it, key):
        cand = key ^ lax.shift_left(jnp.int32(1), 31 - it)
        cnt = jnp.sum(jnp.where(score >= key_to_float(cand), 1.0, 0.0), axis=1, keepdims=True)
        return jnp.where(cnt >= topk, cand, key)

    key = lax.fori_loop(0, 32, search, jnp.full((tq, 1), INT_MIN, jnp.int32))
    thr = key_to_float(jnp.maximum(key, INT_MIN + 0x00800000))
    ge = score >= thr
    cnt = jnp.sum(jnp.where(ge, 1.0, 0.0), axis=1, keepdims=True)
    bias_ref[...] = jnp.where(ge, 0.0, -jnp.inf).astype(F32)

    @pl.when(jnp.max(cnt) > topk)
    def _():
        tw = 2 * LANES
        need = topk - jnp.sum(jnp.where(score > thr, 1.0, 0.0), axis=1, keepdims=True)
        tri = jnp.where(lax.broadcasted_iota(jnp.int32, (tw, tw), 0)
                        <= lax.broadcasted_iota(jnp.int32, (tw, tw), 1), 1.0, 0.0).astype(BF16)
        carry = jnp.zeros((tq, 1), F32)
        for c in range(s_eff // tw):
            kc = score[:, c * tw:(c + 1) * tw]
            eqc = jnp.where(kc == thr, 1.0, 0.0)
            rank = _dot(eqc.astype(BF16), tri) + carry
            take = (kc > thr) | ((kc == thr) & (rank <= need))
            bias_ref[:, c * tw:(c + 1) * tw] = jnp.where(take, 0.0, -jnp.inf).astype(F32)
            carry = carry + jnp.sum(eqc, axis=1, keepdims=True)

    bias = bias_ref[...]
    q = q_ref[...]
    ones = jnp.ones((s_eff, HEAD_DIM), BF16)
    for g in range(N_KV_HEADS):
        qg = jnp.concatenate(
            [q[:, (g * GROUP + h) * HEAD_DIM:(g * GROUP + h + 1) * HEAD_DIM] for h in range(GROUP)],
            axis=0)
        s = _dot_nt(qg, k_ref[:s_eff, g * HEAD_DIM:(g + 1) * HEAD_DIM])
        s = s.reshape(GROUP, tq, s_eff) + bias[None]
        p = jnp.exp2(s - jnp.max(s, axis=-1, keepdims=True)).astype(BF16)
        v_aug = jnp.concatenate([v_ref[:s_eff, g * HEAD_DIM:(g + 1) * HEAD_DIM], ones], axis=1)
        o = _dot(p.reshape(GROUP * tq, s_eff), v_aug)
        o = o[:, :HEAD_DIM] / o[:, HEAD_DIM:]
        for h in range(GROUP):
            col = (g * GROUP + h) * HEAD_DIM
            o_ref[:, col:col + HEAD_DIM] = o[h * tq:(h + 1) * tq].astype(o_ref.dtype)


def _dsa(qkv, qik, gw, *, batch, seq, ck=512):
    m = qkv.shape[0]
    topk = min(TOPK_MAX, seq // 4)
    kv_w = N_KV_HEADS * HEAD_DIM
    qi_w = N_IDX_HEADS * IDX_DIM
    y = jnp.zeros((m, D_MODEL), BF16)
    for n in range(1, seq // ck + 1):
        tq = ck if 2 * n * ck <= seq else ck // 2
        nq = seq // tq
        per = ck // tq
        s_eff, tile0 = n * ck, (n - 1) * per
        row = lambda b, i, tile0=tile0, nq=nq: b * nq + tile0 + i
        y = pl.pallas_call(
            functools.partial(_dsa_kernel, tq=tq, ck=ck, topk=topk, s_eff=s_eff, tile0=tile0),
            grid=(batch, per),
            in_specs=[
                pl.BlockSpec((tq, qi_w), lambda b, i, row=row: (row(b, i), 0)),
                pl.BlockSpec((seq, LANES), lambda b, i: (b, qi_w // LANES)),
                pl.BlockSpec((seq, LANES), lambda b, i: (b, qi_w // LANES + 1)),
                pl.BlockSpec((tq, LANES), lambda b, i, row=row: (row(b, i), 2 * D_MODEL // LANES)),
                pl.BlockSpec((tq, D_MODEL), lambda b, i, row=row: (row(b, i), 0)),
                pl.BlockSpec((seq, kv_w), lambda b, i: (b, D_MODEL // kv_w)),
                pl.BlockSpec((seq, kv_w), lambda b, i: (b, D_MODEL // kv_w + 1)),
                pl.BlockSpec(memory_space=pl.ANY),
            ],
            out_specs=pl.BlockSpec((tq, D_MODEL), lambda b, i, row=row: (row(b, i), 0)),
            out_shape=jax.ShapeDtypeStruct((m, D_MODEL), BF16),
            input_output_aliases={7: 0},
            scratch_shapes=[pltpu.VMEM((tq, s_eff), F32), pltpu.VMEM((tq, s_eff), F32)],
            compiler_params=pltpu.CompilerParams(
                dimension_semantics=("parallel", "arbitrary"),
                vmem_limit_bytes=VMEM_LIMIT_BYTES),
            name=f"dsa_{s_eff}",
        )(qik, qik, qik, gw, qkv, qkv, qkv, y)
    return y


def _merge_ln_kernel(ya_ref, yr_ref, ga_ref, gr_ref, wa_ref, wr_ref, wo_ref, h_ref, g_ref, b_ref, o_ref):
    j = pl.program_id(1)

    @pl.when(j == 0)
    def _():
        o_ref[...] = jnp.zeros_like(o_ref)

    merged = (jax.nn.sigmoid(ga_ref[...]) * _dot(ya_ref[...], wa_ref[...])
              + jax.nn.sigmoid(gr_ref[...]) * _dot(yr_ref[...], wr_ref[...]))
    o_ref[...] += _dot(merged.astype(BF16), wo_ref[...])

    @pl.when(j == pl.num_programs(1) - 1)
    def _():
        y = DN_ALPHA * h_ref[...] + o_ref[...]
        o_ref[...] = _layer_norm(y, g_ref[...], b_ref[...])


def _merge_ln(y_attn, y_rnn, gw, w_attn, w_rnn, w_out, h, g, b, *, tm=512, tn=512):
    m, d = h.shape
    nn = d // tn
    return pl.pallas_call(
        _merge_ln_kernel,
        grid=(m // tm, nn),
        in_specs=[
            pl.BlockSpec((tm, y_attn.shape[1]), lambda i, j: (i, 0)),
            pl.BlockSpec((tm, y_rnn.shape[1]), lambda i, j: (i, 0)),
            pl.BlockSpec((tm, tn), lambda i, j: (i, j)),
            pl.BlockSpec((tm, tn), lambda i, j: (i, j + nn)),
            pl.BlockSpec((w_attn.shape[0], tn), lambda i, j: (0, j)),
            pl.BlockSpec((w_rnn.shape[0], tn), lambda i, j: (0, j)),
            pl.BlockSpec((tn, d), lambda i, j: (j, 0)),
            pl.BlockSpec((tm, d), lambda i, j: (i, 0)),
            pl.BlockSpec((1, d), lambda i, j: (0, 0)),
            pl.BlockSpec((1, d), lambda i, j: (0, 0)),
        ],
        out_specs=pl.BlockSpec((tm, d), lambda i, j: (i, 0)),
        out_shape=jax.ShapeDtypeStruct((m, d), F32),
        compiler_params=pltpu.CompilerParams(
            dimension_semantics=("parallel", "arbitrary"),
            vmem_limit_bytes=VMEM_LIMIT_BYTES),
        name="merge_ln",
    )(y_attn, y_rnn, gw, gw, w_attn, w_rnn, w_out, h, g, b)


def _block_diag_groups(w):
    per = LRU_BLOCKS // LRU_GROUPS
    w4 = w.reshape(LRU_GROUPS, per, LRU_BLOCK, LRU_BLOCK)
    bd = jnp.einsum("gaij,ab->gaibj", w4, jnp.eye(per, dtype=w.dtype))
    return bd.reshape(LRU_GROUPS, LRU_GROUP_W, LRU_GROUP_W)


def kernel(x, positions, ffn1_w_in, ffn1_w_out, ln1_g, ln1_b, w_in, conv_w, conv_b, lru_wa, lru_ba,
           lru_wx, lru_bx, lru_lambda, w_attn_branch, w_rnn_branch, w_out, ln2_g, ln2_b, ffn2_w_in,
           ffn2_w_out, ln3_g, ln3_b):
    batch, seq, d = x.shape
    m = batch * seq
    h = x.reshape(m, d)
    pos_col = positions.reshape(m, 1)
    tab32, tab16 = _rope_tables(pos_col)

    for l in range(DEPTH):
        h, hb = _ffn_ln(h, ffn1_w_in[l].astype(BF16), ffn1_w_out[l].astype(BF16), ln1_g[l][None],
                        ln1_b[l][None], emit_bf16=True)

        wl = w_in[l]
        o_q, o_k, o_v = 0, D_MODEL, D_MODEL + N_KV_HEADS * HEAD_DIM
        o_qi = o_v + N_KV_HEADS * HEAD_DIM
        o_ki = o_qi + N_IDX_HEADS * IDX_DIM
        o_wi = o_ki + IDX_DIM
        o_rx = o_wi + N_IDX_HEADS
        o_gt = o_rx + 2 * D_RNN
        zk = jnp.zeros((d, IDX_DIM), wl.dtype)
        w_ki = wl[:, o_ki:o_wi]
        w_qkv = jnp.concatenate([_pair_layout(wl[:, o_q:o_v], idx_pairs=False), wl[:, o_v:o_qi]],
                                axis=1).astype(BF16)
        w_qik = _pair_layout(jnp.concatenate([wl[:, o_qi:o_ki], w_ki, zk, zk, w_ki], axis=1),
                             idx_pairs=True).astype(BF16)
        w_rxg = wl[:, o_rx:o_gt].astype(BF16)
        pad = jnp.zeros((d, 512 - N_IDX_HEADS), wl.dtype)
        w_gw = jnp.concatenate([wl[:, o_gt:], wl[:, o_wi:o_rx], pad], axis=1).astype(BF16)

        qkv = _proj(hb, w_qkv, BF16, tm=1024, tn=512, tab=tab32, rope_shift=HALF_TILE,
                    n_rope_tiles=(D_MODEL + N_KV_HEADS * HEAD_DIM) // 512,
                    n_scale_tiles=D_MODEL // 512, scale=HEAD_DIM ** -0.5 * LOG2_E, name="proj_qkv")
        qik = _proj(hb, w_qik, BF16, tm=1024, tn=256, tab=tab16, rope_shift=HALF_TILE,
                    n_rope_tiles=w_qik.shape[1] // 256, name="proj_idx")
        rxg = _proj(hb, w_rxg, F32, tm=1024, tn=1024, name="proj_rnn")
        gw = _proj(hb, w_gw, F32, tm=1024, tn=1536, name="proj_gate")

        y_rnn = _rglru(rxg, conv_w[l], conv_b[l][None],
                       _block_diag_groups(lru_wa[l]).astype(BF16), _block_diag_groups(lru_wx[l]).astype(BF16),
                       lru_ba[l][None], lru_bx[l][None], lru_lambda[l][None], batch=batch, seq=seq)
        y_attn = _dsa(qkv, qik, gw, batch=batch, seq=seq)

        h = _merge_ln(y_attn, y_rnn, gw, w_attn_branch[l].astype(BF16), w_rnn_branch[l].astype(BF16),
                      w_out[l].astype(BF16), h, ln2_g[l][None], ln2_b[l][None])
        h = _ffn_ln(h, ffn2_w_in[l].astype(BF16), ffn2_w_out[l].astype(BF16), ln3_g[l][None], ln3_b[l][None],
                    emit_bf16=False)
    return h.reshape(batch, seq, d)
```

```python
import functools

import numpy as np
import jax
import jax.numpy as jnp
from jax import lax
from jax.experimental import pallas as pl
from jax.experimental.pallas import tpu as pltpu

F32 = jnp.float32
BF16 = jnp.bfloat16

D_MODEL = 2048
HEAD_DIM = 128
N_HEADS = D_MODEL // HEAD_DIM
N_KV_HEADS = 4
GROUP = N_HEADS // N_KV_HEADS
ROPE_DIM = HEAD_DIM // 4
ROPE_THETA = 500000.0
N_IDX_HEADS = 16
IDX_DIM = 64
IDX_ROPE_DIM = IDX_DIM // 4
TOPK_MAX = 256
D_RNN = (4 * D_MODEL // 3) // 256 * 256
LRU_BLOCKS = 16
LRU_BLOCK = D_RNN // LRU_BLOCKS
CONV_WIDTH = 4
LRU_C = 8.0
D_FF = (8 * D_MODEL // 3 + 255) // 256 * 256
LN_EPS = 1e-5
DEPTH = 1
DN_ALPHA = (2.0 * DEPTH) ** 0.25

LANES = 128
SUBLANES = 8
MXU_N = 256
VMEM_LIMIT_BYTES = 56 * 1024 * 1024

LRU_GROUPS = 4
LRU_GROUP_W = D_RNN // LRU_GROUPS

INT_MIN = -(2 ** 31)
LOG2_E = 1.4426950408889634


def _layer_norm(y, g, b):
    mu = jnp.mean(y, axis=-1, keepdims=True)
    d = y - mu
    var = jnp.mean(d * d, axis=-1, keepdims=True)
    return d * lax.rsqrt(var + LN_EPS) * g + b


def _dot(a, b):
    return jnp.dot(a, b, preferred_element_type=F32)


def _dot_nt(a, b):
    return lax.dot_general(a, b, (((1,), (1,)), ((), ())), preferred_element_type=F32)


def _ffn_ln_kernel(x_ref, wa_ref, wb_ref, wo_ref, g_ref, b_ref, o_ref, *rest):
    xb_ref = rest[-1]
    j = pl.program_id(1)

    @pl.when(j == 0)
    def _():
        xb_ref[...] = x_ref[...].astype(BF16)
        o_ref[...] = jnp.zeros_like(o_ref)

    xb = xb_ref[...]
    a = _dot(xb, wa_ref[...])
    b = _dot(xb, wb_ref[...])
    act = (jax.nn.silu(a) * b).astype(BF16)
    o_ref[...] += _dot(act, wo_ref[...])

    @pl.when(j == pl.num_programs(1) - 1)
    def _():
        y = DN_ALPHA * x_ref[...] + 0.5 * o_ref[...]
        out = _layer_norm(y, g_ref[...], b_ref[...])
        o_ref[...] = out
        if len(rest) == 2:
            rest[0][...] = out.astype(BF16)


def _ffn_ln(x, w_in, w_out, g, b, *, emit_bf16, tm=512, tf=512):
    m, d = x.shape
    f = w_out.shape[0]
    nf = f // tf
    out_spec = pl.BlockSpec((tm, d), lambda i, j: (i, 0))
    out_specs, out_shape = out_spec, jax.ShapeDtypeStruct((m, d), F32)
    if emit_bf16:
        out_specs, out_shape = [out_spec, out_spec], [out_shape, jax.ShapeDtypeStruct((m, d), BF16)]
    return pl.pallas_call(
        _ffn_ln_kernel,
        grid=(m // tm, nf),
        in_specs=[
            pl.BlockSpec((tm, d), lambda i, j: (i, 0)),
            pl.BlockSpec((d, tf), lambda i, j: (0, j)),
            pl.BlockSpec((d, tf), lambda i, j: (0, j + nf)),
            pl.BlockSpec((tf, d), lambda i, j: (j, 0)),
            pl.BlockSpec((1, d), lambda i, j: (0, 0)),
            pl.BlockSpec((1, d), lambda i, j: (0, 0)),
        ],
        out_specs=out_specs,
        out_shape=out_shape,
        scratch_shapes=[pltpu.VMEM((tm, d), BF16)],
        compiler_params=pltpu.CompilerParams(
            dimension_semantics=("parallel", "arbitrary"),
            vmem_limit_bytes=VMEM_LIMIT_BYTES),
        name="ffn_ln",
    )(x, w_in, w_in, w_out, g, b)


def _rope_tab_kernel(pos_ref, f_ref, o32_ref, o16_ref):
    pos = pos_ref[...].astype(F32)
    for row, o_ref in ((0, o32_ref), (3, o16_ref)):
        ang = pos * f_ref[row:row + 1, :]
        s = jnp.sin(ang)
        o_ref[0] = jnp.cos(ang)
        o_ref[1] = s * f_ref[row + 1:row + 2, :]
        o_ref[2] = s * f_ref[row + 2:row + 3, :]


def _rope_lane_table():
    lane = np.arange(LANES)

    def rows(rot_dim, period):
        half = rot_dim // 2
        inv_freq = jnp.power(ROPE_THETA, -(jnp.arange(half, dtype=F32) * 2.0 / rot_dim))
        l = lane % period
        freq = jnp.where(l < rot_dim, inv_freq[l % half], 0.0)
        plus = ((l >= half) & (l < rot_dim)).astype(np.float32)
        minus = -(l < half).astype(np.float32)
        return [freq, jnp.asarray(plus), jnp.asarray(minus)]

    z = jnp.zeros((LANES,), F32)
    return jnp.stack(rows(ROPE_DIM, HEAD_DIM) + rows(IDX_ROPE_DIM, IDX_DIM) + [z, z]).astype(F32)


def _rope_tables(pos_col, *, tm=1024):
    m = pos_col.shape[0]
    shp = jax.ShapeDtypeStruct((3, m, LANES), F32)
    return pl.pallas_call(
        _rope_tab_kernel,
        grid=(m // tm,),
        in_specs=[pl.BlockSpec((tm, 1), lambda i: (i, 0)),
                  pl.BlockSpec((8, LANES), lambda i: (0, 0))],
        out_specs=[pl.BlockSpec((3, tm, LANES), lambda i: (0, i, 0)),
                   pl.BlockSpec((3, tm, LANES), lambda i: (0, i, 0))],
        out_shape=[shp, shp],
        compiler_params=pltpu.CompilerParams(dimension_semantics=("parallel",)),
        name="rope_tables",
    )(pos_col, _rope_lane_table())


def _proj_kernel(x_ref, w_ref, *rest, tn, rope_shift, n_rope_tiles, n_scale_tiles, scale):
    if rope_shift is None:
        (o_ref,) = rest
        o_ref[...] = _dot(x_ref[...], w_ref[...]).astype(o_ref.dtype)
        return
    tab_ref, o_ref = rest
    j = pl.program_id(1)
    roped = j < n_rope_tiles
    sc = jnp.where(j < n_scale_tiles, scale, 1.0).astype(F32)
    c = jnp.where(roped, tab_ref[0], 1.0) * sc
    s_hi = jnp.where(roped, tab_ref[1], 0.0) * sc
    s_lo = jnp.where(roped, tab_ref[2], 0.0) * sc
    x = x_ref[...]
    for u in range(tn // MXU_N):
        acc = _dot(x, w_ref[:, u * MXU_N:(u + 1) * MXU_N])
        for t in range(MXU_N // LANES):
            a = acc[:, t * LANES:(t + 1) * LANES]
            r = (a * c + pltpu.roll(a, rope_shift, 1) * s_hi
                 + pltpu.roll(a, LANES - rope_shift, 1) * s_lo)
            col = u * MXU_N + t * LANES
            o_ref[:, col:col + LANES] = r.astype(o_ref.dtype)


def _proj(xb, w, out_dtype, *, tm, tn, tab=None, rope_shift=None, n_rope_tiles=0,
          n_scale_tiles=0, scale=1.0, name="proj"):
    m, d = xb.shape
    n = w.shape[1]
    in_specs = [pl.BlockSpec((tm, d), lambda i, j: (i, 0)),
                pl.BlockSpec((d, tn), lambda i, j: (0, j))]
    args = [xb, w]
    if rope_shift is not None:
        in_specs.append(pl.BlockSpec((3, tm, LANES), lambda i, j: (0, i, 0)))
        args.append(tab)
    return pl.pallas_call(
        functools.partial(_proj_kernel, tn=tn, rope_shift=rope_shift, n_rope_tiles=n_rope_tiles,
                          n_scale_tiles=n_scale_tiles, scale=scale),
        grid=(m // tm, n // tn),
        in_specs=in_specs,
        out_specs=pl.BlockSpec((tm, tn), lambda i, j: (i, j)),
        out_shape=jax.ShapeDtypeStruct((m, n), out_dtype),
        compiler_params=pltpu.CompilerParams(
            dimension_semantics=("parallel", "arbitrary"),
            vmem_limit_bytes=VMEM_LIMIT_BYTES),
        name=name,
    )(*args)


def _rglru_kernel(rx_ref, rg_ref, cw_ref, cb_ref, wa_ref, wx_ref, ba_ref, bx_ref, lam_ref,
                  o_ref, xbuf, a_s, b_s, h_s, *, tt):
    t = pl.program_id(2)
    w = xbuf.shape[1]

    @pl.when(t == 0)
    def _():
        xbuf[0:SUBLANES, :] = jnp.zeros((SUBLANES, w), F32)
        h_s[...] = jnp.zeros_like(h_s)

    xbuf[SUBLANES:SUBLANES + tt, :] = rx_ref[...]
    xc = cb_ref[...]
    for k in range(CONV_WIDTH):
        off = SUBLANES - (CONV_WIDTH - 1) + k
        xc = xc + cw_ref[k:k + 1, :] * xbuf[off:off + tt, :]
    xbuf[0:SUBLANES, :] = rx_ref[tt - SUBLANES:tt, :]

    xcb = xc.astype(BF16)
    r = jax.nn.sigmoid(_dot(xcb, wa_ref[0]) + ba_ref[...])
    ig = jax.nn.sigmoid(_dot(xcb, wx_ref[0]) + bx_ref[...])
    log_a = -LRU_C * r * jax.nn.softplus(-lam_ref[...])
    a = jnp.exp(log_a)
    a_s[...] = a
    b_s[...] = jnp.sqrt(-jnp.tanh(log_a) * (a * a + 1.0)) * (ig * xc)

    row = lax.broadcasted_iota(jnp.int32, (SUBLANES, w), 0)

    def body(g, h):
        sl = pl.ds(pl.multiple_of(g * SUBLANES, SUBLANES), SUBLANES)
        a8 = a_s[sl, :]
        b8 = b_s[sl, :]
        for d in (1, 2, 4):
            keep = row >= d
            a_sh = jnp.where(keep, pltpu.roll(a8, d, 0), 1.0)
            b_sh = jnp.where(keep, pltpu.roll(b8, d, 0), 0.0)
            b8 = a8 * b_sh + b8
            a8 = a8 * a_sh
        hrows = a8 * h + b8
        b_s[sl, :] = hrows
        return jnp.broadcast_to(hrows[SUBLANES - 1:SUBLANES, :], (SUBLANES, w))

    h_s[...] = lax.fori_loop(0, tt // SUBLANES, body, h_s[...])
    o_ref[...] = (b_s[...] * jax.nn.gelu(rg_ref[...])).astype(o_ref.dtype)


def _rglru(rxg, conv_w, conv_b, wa_bd, wx_bd, ba, bx, lam, *, batch, seq, tt=512):
    m = rxg.shape[0]
    nt = seq // tt
    gw = LRU_GROUP_W
    vec = lambda: pl.BlockSpec((1, gw), lambda b, g, t: (0, g))
    return pl.pallas_call(
        functools.partial(_rglru_kernel, tt=tt),
        grid=(batch, LRU_GROUPS, nt),
        in_specs=[
            pl.BlockSpec((tt, gw), lambda b, g, t: (b * nt + t, g)),
            pl.BlockSpec((tt, gw), lambda b, g, t: (b * nt + t, g + LRU_GROUPS)),
            pl.BlockSpec((CONV_WIDTH, gw), lambda b, g, t: (0, g)),
            vec(),
            pl.BlockSpec((1, gw, gw), lambda b, g, t: (g, 0, 0)),
            pl.BlockSpec((1, gw, gw), lambda b, g, t: (g, 0, 0)),
            vec(), vec(), vec(),
        ],
        out_specs=pl.BlockSpec((tt, gw), lambda b, g, t: (b * nt + t, g)),
        out_shape=jax.ShapeDtypeStruct((m, D_RNN), BF16),
        scratch_shapes=[pltpu.VMEM((tt + SUBLANES, gw), F32), pltpu.VMEM((tt, gw), F32),
                        pltpu.VMEM((tt, gw), F32), pltpu.VMEM((SUBLANES, gw), F32)],
        compiler_params=pltpu.CompilerParams(
            dimension_semantics=("parallel", "parallel", "arbitrary"),
            vmem_limit_bytes=VMEM_LIMIT_BYTES),
        name="rglru",
    )(rxg, rxg, conv_w, conv_b, wa_bd, wx_bd, ba, bx, lam)


def _dsa_kernel(qi_ref, kie_ref, kio_ref, wi_ref, q_ref, k_ref, v_ref, prev_ref, o_ref,
                score_ref, bias_ref, *, tq, ck, topk, s_eff, tile0):
    del prev_ref
    i = tile0 + pl.program_id(1)
    n_pair = N_IDX_HEADS // 2

    qi = qi_ref[...]
    qs = jnp.concatenate([qi[:, j * LANES:(j + 1) * LANES] for j in range(n_pair)], axis=0)
    w = wi_ref[...] * (N_IDX_HEADS ** -0.5 * IDX_DIM ** -0.5)
    for c in range(s_eff // ck):
        le = _dot_nt(qs, kie_ref[c * ck:(c + 1) * ck, :])
        lo = _dot_nt(qs, kio_ref[c * ck:(c + 1) * ck, :])
        acc = jnp.zeros((tq, ck), F32)
        for j in range(n_pair):
            acc = acc + jnp.maximum(le[j * tq:(j + 1) * tq], 0.0) * w[:, 2 * j:2 * j + 1]
            acc = acc + jnp.maximum(lo[j * tq:(j + 1) * tq], 0.0) * w[:, 2 * j + 1:2 * j + 2]
        score_ref[:, c * ck:(c + 1) * ck] = acc

    qpos = i * tq + lax.broadcasted_iota(jnp.int32, (tq, s_eff), 0)
    kpos = lax.broadcasted_iota(jnp.int32, (tq, s_eff), 1)
    score = jnp.where(kpos <= qpos, score_ref[...], -jnp.inf)

    def key_to_float(key):
        return pltpu.bitcast(jnp.where(key < 0, key ^ 0x7FFFFFFF, key), F32)

    def search(it, key):
        cand = key ^ lax.shift_left(jnp.int32(1), 31 - it)
        cnt = jnp.sum(jnp.where(score >= key_to_float(cand), 1.0, 0.0), axis=1, keepdims=True)
        return jnp.where(cnt >= topk, cand, key)

    key = lax.fori_loop(0, 32, search, jnp.full((tq, 1), INT_MIN, jnp.int32))
    thr = key_to_float(jnp.maximum(key, INT_MIN + 0x00800000))
    ge = score >= thr
    cnt = jnp.sum(jnp.where(ge, 1.0, 0.0), axis=1, keepdims=True)
    bias_ref[...] = jnp.where(ge, 0.0, -jnp.inf).astype(F32)

    @pl.when(jnp.max(cnt) > topk)
    def _():
        tw = 2 * LANES
        need = topk - jnp.sum(jnp.where(score > thr, 1.0, 0.0), axis=1, keepdims=True)
        tri = jnp.where(lax.broadcasted_iota(jnp.int32, (tw, tw), 0)
                        <= lax.broadcasted_iota(jnp.int32, (tw, tw), 1), 1.0, 0.0).astype(BF16)
        carry = jnp.zeros((tq, 1), F32)
        for c in range(s_eff // tw):
            kc = score[:, c * tw:(c + 1) * tw]
            eqc = jnp.where(kc == thr, 1.0, 0.0)
            rank = _dot(eqc.astype(BF16), tri) + carry
            take = (kc > thr) | ((kc == thr) & (rank <= need))
            bias_ref[:, c * tw:(c + 1) * tw] = jnp.where(take, 0.0, -jnp.inf).astype(F32)
            carry = carry + jnp.sum(eqc, axis=1, keepdims=True)

    bias = bias_ref[...]
    q = q_ref[...]
    ones = jnp.ones((s_eff, HEAD_DIM), BF16)
    for g in range(N_KV_HEADS):
        qg = jnp.concatenate(
            [q[:, (g * GROUP + h) * HEAD_DIM:(g * GROUP + h + 1) * HEAD_DIM] for h in range(GROUP)],
            axis=0)
        s = _dot_nt(qg, k_ref[:s_eff, g * HEAD_DIM:(g + 1) * HEAD_DIM])
        s = s.reshape(GROUP, tq, s_eff) + bias[None]
        p = jnp.exp2(s - jnp.max(s, axis=-1, keepdims=True)).astype(BF16)
        v_aug = jnp.concatenate([v_ref[:s_eff, g * HEAD_DIM:(g + 1) * HEAD_DIM], ones], axis=1)
        o = _dot(p.reshape(GROUP * tq, s_eff), v_aug)
        o = o[:, :HEAD_DIM] / o[:, HEAD_DIM:]
        for h in range(GROUP):
            col = (g * GROUP + h) * HEAD_DIM
            o_ref[:, col:col + HEAD_DIM] = o[h * tq:(h + 1) * tq].astype(o_ref.dtype)


def _dsa(qkv, qik, gw, *, batch, seq, ck=512):
    m = qkv.shape[0]
    topk = min(TOPK_MAX, seq // 4)
    kv_w = N_KV_HEADS * HEAD_DIM
    qi_w = N_IDX_HEADS * IDX_DIM
    y = jnp.zeros((m, D_MODEL), BF16)
    for n in range(1, seq // ck + 1):
        tq = ck if 2 * n * ck <= seq else ck // 2
        nq = seq // tq
        per = ck // tq
        s_eff, tile0 = n * ck, (n - 1) * per
        row = lambda b, i, tile0=tile0, nq=nq: b * nq + tile0 + i
        y = pl.pallas_call(
            functools.partial(_dsa_kernel, tq=tq, ck=ck, topk=topk, s_eff=s_eff, tile0=tile0),
            grid=(batch, per),
            in_specs=[
                pl.BlockSpec((tq, qi_w), lambda b, i, row=row: (row(b, i), 0)),
                pl.BlockSpec((seq, LANES), lambda b, i: (b, qi_w // LANES)),
                pl.BlockSpec((seq, LANES), lambda b, i: (b, qi_w // LANES + 1)),
                pl.BlockSpec((tq, LANES), lambda b, i, row=row: (row(b, i), 2 * D_MODEL // LANES)),
                pl.BlockSpec((tq, D_MODEL), lambda b, i, row=row: (row(b, i), 0)),
                pl.BlockSpec((seq, kv_w), lambda b, i: (b, D_MODEL // kv_w)),
                pl.BlockSpec((seq, kv_w), lambda b, i: (b, D_MODEL // kv_w + 1)),
                pl.BlockSpec(memory_space=pl.ANY),
            ],
            out_specs=pl.BlockSpec((tq, D_MODEL), lambda b, i, row=row: (row(b, i), 0)),
            out_shape=jax.ShapeDtypeStruct((m, D_MODEL), BF16),
            input_output_aliases={7: 0},
            scratch_shapes=[pltpu.VMEM((tq, s_eff), F32), pltpu.VMEM((tq, s_eff), F32)],
            compiler_params=pltpu.CompilerParams(
                dimension_semantics=("parallel", "arbitrary"),
                vmem_limit_bytes=VMEM_LIMIT_BYTES),
            name=f"dsa_{s_eff}",
        )(qik, qik, qik, gw, qkv, qkv, qkv, y)
    return y


def _merge_ln_kernel(ya_ref, yr_ref, ga_ref, gr_ref, wa_ref, wr_ref, wo_ref, h_ref, g_ref, b_ref, o_ref):
    j = pl.program_id(1)

    @pl.when(j == 0)
    def _():
        o_ref[...] = jnp.zeros_like(o_ref)

    merged = (jax.nn.sigmoid(ga_ref[...]) * _dot(ya_ref[...], wa_ref[...])
              + jax.nn.sigmoid(gr_ref[...]) * _dot(yr_ref[...], wr_ref[...]))
    o_ref[...] += _dot(merged.astype(BF16), wo_ref[...])

    @pl.when(j == pl.num_programs(1) - 1)
    def _():
        y = DN_ALPHA * h_ref[...] + o_ref[...]
        o_ref[...] = _layer_norm(y, g_ref[...], b_ref[...])


def _merge_ln(y_attn, y_rnn, gw, w_attn, w_rnn, w_out, h, g, b, *, tm=512, tn=512):
    m, d = h.shape
    nn = d // tn
    return pl.pallas_call(
        _merge_ln_kernel,
        grid=(m // tm, nn),
        in_specs=[
            pl.BlockSpec((tm, y_attn.shape[1]), lambda i, j: (i, 0)),
            pl.BlockSpec((tm, y_rnn.shape[1]), lambda i, j: (i, 0)),
            pl.BlockSpec((tm, tn), lambda i, j: (i, j)),
            pl.BlockSpec((tm, tn), lambda i, j: (i, j + nn)),
            pl.BlockSpec((w_attn.shape[0], tn), lambda i, j: (0, j)),
            pl.BlockSpec((w_rnn.shape[0], tn), lambda i, j: (0, j)),
            pl.BlockSpec((tn, d), lambda i, j: (j, 0)),
            pl.BlockSpec((tm, d), lambda i, j: (i, 0)),
            pl.BlockSpec((1, d), lambda i, j: (0, 0)),
            pl.BlockSpec((1, d), lambda i, j: (0, 0)),
        ],
        out_specs=pl.BlockSpec((tm, d), lambda i, j: (i, 0)),
        out_shape=jax.ShapeDtypeStruct((m, d), F32),
        compiler_params=pltpu.CompilerParams(
            dimension_semantics=("parallel", "arbitrary"),
            vmem_limit_bytes=VMEM_LIMIT_BYTES),
        name="merge_ln",
    )(y_attn, y_rnn, gw, gw, w_attn, w_rnn, w_out, h, g, b)


def _block_diag_groups(w):
    per = LRU_BLOCKS // LRU_GROUPS
    w4 = w.reshape(LRU_GROUPS, per, LRU_BLOCK, LRU_BLOCK)
    bd = jnp.einsum("gaij,ab->gaibj", w4, jnp.eye(per, dtype=w.dtype))
    return bd.reshape(LRU_GROUPS, LRU_GROUP_W, LRU_GROUP_W)


def kernel(x, positions, ffn1_w_in, ffn1_w_out, ln1_g, ln1_b, w_in, conv_w, conv_b, lru_wa, lru_ba,
           lru_wx, lru_bx, lru_lambda, w_attn_branch, w_rnn_branch, w_out, ln2_g, ln2_b, ffn2_w_in,
           ffn2_w_out, ln3_g, ln3_b):
    batch, seq, d = x.shape
    m = batch * seq
    h = x.reshape(m, d)
    pos_col = positions.reshape(m, 1)
    tab32, tab16 = _rope_tables(pos_col)

    for l in range(DEPTH):
        h, hb = _ffn_ln(h, ffn1_w_in[l].astype(BF16), ffn1_w_out[l].astype(BF16), ln1_g[l][None],
                        ln1_b[l][None], emit_bf16=True)

        wl = w_in[l]
        o_q, o_k, o_v = 0, D_MODEL, D_MODEL + N_KV_HEADS * HEAD_DIM
        o_qi = o_v + N_KV_HEADS * HEAD_DIM
        o_ki = o_qi + N_IDX_HEADS * IDX_DIM
        o_wi = o_ki + IDX_DIM
        o_rx = o_wi + N_IDX_HEADS
        o_gt = o_rx + 2 * D_RNN
        zk = jnp.zeros((d, IDX_DIM), wl.dtype)
        w_ki = wl[:, o_ki:o_wi]
        w_qkv = wl[:, o_q:o_qi].astype(BF16)
        w_qik = jnp.concatenate([wl[:, o_qi:o_ki], w_ki, zk, zk, w_ki], axis=1).astype(BF16)
        w_rxg = wl[:, o_rx:o_gt].astype(BF16)
        pad = jnp.zeros((d, 512 - N_IDX_HEADS), wl.dtype)
        w_gw = jnp.concatenate([wl[:, o_gt:], wl[:, o_wi:o_rx], pad], axis=1).astype(BF16)

        qkv = _proj(hb, w_qkv, BF16, tm=1024, tn=512, tab=tab32, rope_shift=ROPE_DIM // 2,
                    n_rope_tiles=(D_MODEL + N_KV_HEADS * HEAD_DIM) // 512,
                    n_scale_tiles=D_MODEL // 512, scale=HEAD_DIM ** -0.5 * LOG2_E, name="proj_qkv")
        qik = _proj(hb, w_qik, BF16, tm=1024, tn=256, tab=tab16, rope_shift=IDX_ROPE_DIM // 2,
                    n_rope_tiles=w_qik.shape[1] // 256, name="proj_idx")
        rxg = _proj(hb, w_rxg, F32, tm=1024, tn=1024, name="proj_rnn")
        gw = _proj(hb, w_gw, F32, tm=1024, tn=1536, name="proj_gate")

        y_rnn = _rglru(rxg, conv_w[l], conv_b[l][None],
                       _block_diag_groups(lru_wa[l]).astype(BF16), _block_diag_groups(lru_wx[l]).astype(BF16),
                       lru_ba[l][None], lru_bx[l][None], lru_lambda[l][None], batch=batch, seq=seq)
        y_attn = _dsa(qkv, qik, gw, batch=batch, seq=seq)

        h = _merge_ln(y_attn, y_rnn, gw, w_attn_branch[l].astype(BF16), w_rnn_branch[l].astype(BF16),
                      w_out[l].astype(BF16), h, ln2_g[l][None], ln2_b[l][None])
        h = _ffn_ln(h, ffn2_w_in[l].astype(BF16), ffn2_w_out[l].astype(BF16), ln3_g[l][None], ln3_b[l][None],
                    emit_bf16=False)
    return h.reshape(batch, seq, d)
```

```python
import functools

import numpy as np
import jax
import jax.numpy as jnp
from jax import lax
from jax.experimental import pallas as pl
from jax.experimental.pallas import tpu as pltpu

F32 = jnp.float32
BF16 = jnp.bfloat16

D_MODEL = 2048
HEAD_DIM = 128
N_HEADS = D_MODEL // HEAD_DIM
N_KV_HEADS = 4
GROUP = N_HEADS // N_KV_HEADS
ROPE_DIM = HEAD_DIM // 4
ROPE_THETA = 500000.0
N_IDX_HEADS = 16
IDX_DIM = 64
IDX_ROPE_DIM = IDX_DIM // 4
TOPK_MAX = 256
D_RNN = (4 * D_MODEL // 3) // 256 * 256
LRU_BLOCKS = 16
LRU_BLOCK = D_RNN // LRU_BLOCKS
CONV_WIDTH = 4
LRU_C = 8.0
D_FF = (8 * D_MODEL // 3 + 255) // 256 * 256
LN_EPS = 1e-5
DEPTH = 1
DN_ALPHA = (2.0 * DEPTH) ** 0.25

LANES = 128
SUBLANES = 8
MXU_N = 256
VMEM_LIMIT_BYTES = 56 * 1024 * 1024

LRU_GROUPS = 4
LRU_GROUP_W = D_RNN // LRU_GROUPS

INT_MIN = -(2 ** 31)
LOG2_E = 1.4426950408889634


def _layer_norm(y, g, b):
    mu = jnp.mean(y, axis=-1, keepdims=True)
    d = y - mu
    var = jnp.mean(d * d, axis=-1, keepdims=True)
    return d * lax.rsqrt(var + LN_EPS) * g + b


def _dot(a, b):
    return jnp.dot(a, b, preferred_element_type=F32)


def _dot_nt(a, b):
    return lax.dot_general(a, b, (((1,), (1,)), ((), ())), preferred_element_type=F32)


def _ffn_ln_kernel(x_ref, wa_ref, wb_ref, wo_ref, g_ref, b_ref, o_ref, *rest):
    xb_ref = rest[-1]
    j = pl.program_id(1)

    @pl.when(j == 0)
    def _():
        xb_ref[...] = x_ref[...].astype(BF16)
        o_ref[...] = jnp.zeros_like(o_ref)

    xb = xb_ref[...]
    a = _dot(xb, wa_ref[...])
    b = _dot(xb, wb_ref[...])
    act = (jax.nn.silu(a) * b).astype(BF16)
    o_ref[...] += _dot(act, wo_ref[...])

    @pl.when(j == pl.num_programs(1) - 1)
    def _():
        y = DN_ALPHA * x_ref[...] + 0.5 * o_ref[...]
        out = _layer_norm(y, g_ref[...], b_ref[...])
        o_ref[...] = out
        if len(rest) == 2:
            rest[0][...] = out.astype(BF16)


def _ffn_ln(x, w_in, w_out, g, b, *, emit_bf16, tm=512, tf=512):
    m, d = x.shape
    f = w_out.shape[0]
    nf = f // tf
    out_spec = pl.BlockSpec((tm, d), lambda i, j: (i, 0))
    out_specs, out_shape = out_spec, jax.ShapeDtypeStruct((m, d), F32)
    if emit_bf16:
        out_specs, out_shape = [out_spec, out_spec], [out_shape, jax.ShapeDtypeStruct((m, d), BF16)]
    return pl.pallas_call(
        _ffn_ln_kernel,
        grid=(m // tm, nf),
        in_specs=[
            pl.BlockSpec((tm, d), lambda i, j: (i, 0)),
            pl.BlockSpec((d, tf), lambda i, j: (0, j)),
            pl.BlockSpec((d, tf), lambda i, j: (0, j + nf)),
            pl.BlockSpec((tf, d), lambda i, j: (j, 0)),
            pl.BlockSpec((1, d), lambda i, j: (0, 0)),
            pl.BlockSpec((1, d), lambda i, j: (0, 0)),
        ],
        out_specs=out_specs,
        out_shape=out_shape,
        scratch_shapes=[pltpu.VMEM((tm, d), BF16)],
        compiler_params=pltpu.CompilerParams(
            dimension_semantics=("parallel", "arbitrary"),
            vmem_limit_bytes=VMEM_LIMIT_BYTES),
        name="ffn_ln",
    )(x, w_in, w_in, w_out, g, b)


def _rope_tab_kernel(pos_ref, f_ref, o32_ref, o16_ref):
    pos = pos_ref[...].astype(F32)
    for row, o_ref in ((0, o32_ref), (3, o16_ref)):
        ang = pos * f_ref[row:row + 1, :]
        s = jnp.sin(ang)
        o_ref[0] = jnp.cos(ang)
        o_ref[1] = s * f_ref[row + 1:row + 2, :]
        o_ref[2] = s * f_ref[row + 2:row + 3, :]


def _rope_lane_table():
    lane = np.arange(LANES)

    def rows(rot_dim, period):
        half = rot_dim // 2
        inv_freq = jnp.power(ROPE_THETA, -(jnp.arange(half, dtype=F32) * 2.0 / rot_dim))
        l = lane % period
        freq = jnp.where(l < rot_dim, inv_freq[l % half], 0.0)
        plus = ((l >= half) & (l < rot_dim)).astype(np.float32)
        minus = -(l < half).astype(np.float32)
        return [freq, jnp.asarray(plus), jnp.asarray(minus)]

    z = jnp.zeros((LANES,), F32)
    return jnp.stack(rows(ROPE_DIM, HEAD_DIM) + rows(IDX_ROPE_DIM, IDX_DIM) + [z, z]).astype(F32)


def _rope_tables(pos_col, *, tm=1024):
    m = pos_col.shape[0]
    shp = jax.ShapeDtypeStruct((3, m, LANES), F32)
    return pl.pallas_call(
        _rope_tab_kernel,
        grid=(m // tm,),
        in_specs=[pl.BlockSpec((tm, 1), lambda i: (i, 0)),
                  pl.BlockSpec((8, LANES), lambda i: (0, 0))],
        out_specs=[pl.BlockSpec((3, tm, LANES), lambda i: (0, i, 0)),
                   pl.BlockSpec((3, tm, LANES), lambda i: (0, i, 0))],
        out_shape=[shp, shp],
        compiler_params=pltpu.CompilerParams(dimension_semantics=("parallel",)),
        name="rope_tables",
    )(pos_col, _rope_lane_table())


def _proj_kernel(x_ref, w_ref, *rest, tn, rope_shift, n_rope_tiles, n_scale_tiles, scale):
    if rope_shift is None:
        (o_ref,) = rest
        o_ref[...] = _dot(x_ref[...], w_ref[...]).astype(o_ref.dtype)
        return
    tab_ref, o_ref = rest
    j = pl.program_id(1)
    roped = j < n_rope_tiles
    sc = jnp.where(j < n_scale_tiles, scale, 1.0).astype(F32)
    c = jnp.where(roped, tab_ref[0], 1.0) * sc
    s_hi = jnp.where(roped, tab_ref[1], 0.0) * sc
    s_lo = jnp.where(roped, tab_ref[2], 0.0) * sc
    x = x_ref[...]
    for u in range(tn // MXU_N):
        acc = _dot(x, w_ref[:, u * MXU_N:(u + 1) * MXU_N])
        for t in range(MXU_N // LANES):
            a = acc[:, t * LANES:(t + 1) * LANES]
            r = (a * c + pltpu.roll(a, rope_shift, 1) * s_hi
                 + pltpu.roll(a, LANES - rope_shift, 1) * s_lo)
            col = u * MXU_N + t * LANES
            o_ref[:, col:col + LANES] = r.astype(o_ref.dtype)


def _proj(xb, w, out_dtype, *, tm, tn, tab=None, rope_shift=None, n_rope_tiles=0,
          n_scale_tiles=0, scale=1.0, name="proj"):
    m, d = xb.shape
    n = w.shape[1]
    in_specs = [pl.BlockSpec((tm, d), lambda i, j: (i, 0)),
                pl.BlockSpec((d, tn), lambda i, j: (0, j))]
    args = [xb, w]
    if rope_shift is not None:
        in_specs.append(pl.BlockSpec((3, tm, LANES), lambda i, j: (0, i, 0)))
        args.append(tab)
    return pl.pallas_call(
        functools.partial(_proj_kernel, tn=tn, rope_shift=rope_shift, n_rope_tiles=n_rope_tiles,
                          n_scale_tiles=n_scale_tiles, scale=scale),
        grid=(m // tm, n // tn),
        in_specs=in_specs,
        out_specs=pl.BlockSpec((tm, tn), lambda i, j: (i, j)),
        out_shape=jax.ShapeDtypeStruct((m, n), out_dtype),
        compiler_params=pltpu.CompilerParams(
            dimension_semantics=("parallel", "arbitrary"),
            vmem_limit_bytes=VMEM_LIMIT_BYTES),
        name=name,
    )(*args)


def _rglru_kernel(rx_ref, rg_ref, cw_ref, cb_ref, wa_ref, wx_ref, ba_ref, bx_ref, lam_ref,
                  o_ref, xbuf, a_s, b_s, h_s, *, tt):
    t = pl.program_id(2)
    w = xbuf.shape[1]

    @pl.when(t == 0)
    def _():
        xbuf[0:SUBLANES, :] = jnp.zeros((SUBLANES, w), F32)
        h_s[...] = jnp.zeros_like(h_s)

    xbuf[SUBLANES:SUBLANES + tt, :] = rx_ref[...]
    xc = cb_ref[...]
    for k in range(CONV_WIDTH):
        off = SUBLANES - (CONV_WIDTH - 1) + k
        xc = xc + cw_ref[k:k + 1, :] * xbuf[off:off + tt, :]
    xbuf[0:SUBLANES, :] = rx_ref[tt - SUBLANES:tt, :]

    xcb = xc.astype(BF16)
    r = jax.nn.sigmoid(_dot(xcb, wa_ref[0]) + ba_ref[...])
    ig = jax.nn.sigmoid(_dot(xcb, wx_ref[0]) + bx_ref[...])
    log_a = -LRU_C * r * jax.nn.softplus(-lam_ref[...])
    a = jnp.exp(log_a)
    a_s[...] = a
    b_s[...] = jnp.sqrt(-jnp.tanh(log_a) * (a * a + 1.0)) * (ig * xc)

    row = lax.broadcasted_iota(jnp.int32, (SUBLANES, w), 0)

    def body(g, h):
        sl = pl.ds(pl.multiple_of(g * SUBLANES, SUBLANES), SUBLANES)
        a8 = a_s[sl, :]
        b8 = b_s[sl, :]
        for d in (1, 2, 4):
            keep = row >= d
            a_sh = jnp.where(keep, pltpu.roll(a8, d, 0), 1.0)
            b_sh = jnp.where(keep, pltpu.roll(b8, d, 0), 0.0)
            b8 = a8 * b_sh + b8
            a8 = a8 * a_sh
        hrows = a8 * h + b8
        b_s[sl, :] = hrows
        return jnp.broadcast_to(hrows[SUBLANES - 1:SUBLANES, :], (SUBLANES, w))

    h_s[...] = lax.fori_loop(0, tt // SUBLANES, body, h_s[...])
    o_ref[...] = (b_s[...] * jax.nn.gelu(rg_ref[...])).astype(o_ref.dtype)


def _rglru(rxg, conv_w, conv_b, wa_bd, wx_bd, ba, bx, lam, *, batch, seq, tt=512):
    m = rxg.shape[0]
    nt = seq // tt
    gw = LRU_GROUP_W
    vec = lambda: pl.BlockSpec((1, gw), lambda b, g, t: (0, g))
    return pl.pallas_call(
        functools.partial(_rglru_kernel, tt=tt),
        grid=(batch, LRU_GROUPS, nt),
        in_specs=[
            pl.BlockSpec((tt, gw), lambda b, g, t: (b * nt + t, g)),
            pl.BlockSpec((tt, gw), lambda b, g, t: (b * nt + t, g + LRU_GROUPS)),
            pl.BlockSpec((CONV_WIDTH, gw), lambda b, g, t: (0, g)),
            vec(),
            pl.BlockSpec((1, gw, gw), lambda b, g, t: (g, 0, 0)),
            pl.BlockSpec((1, gw, gw), lambda b, g, t: (g, 0, 0)),
            vec(), vec(), vec(),
        ],
        out_specs=pl.BlockSpec((tt, gw), lambda b, g, t: (b * nt + t, g)),
        out_shape=jax.ShapeDtypeStruct((m, D_RNN), BF16),
        scratch_shapes=[pltpu.VMEM((tt + SUBLANES, gw), F32), pltpu.VMEM((tt, gw), F32),
                        pltpu.VMEM((tt, gw), F32), pltpu.VMEM((SUBLANES, gw), F32)],
        compiler_params=pltpu.CompilerParams(
            dimension_semantics=("parallel", "parallel", "arbitrary"),
            vmem_limit_bytes=VMEM_LIMIT_BYTES),
        name="rglru",
    )(rxg, rxg, conv_w, conv_b, wa_bd, wx_bd, ba, bx, lam)


def _dsa_kernel(qi_ref, kie_ref, kio_ref, wi_ref, q_ref, k_ref, v_ref, prev_ref, o_ref,
                score_ref, bias_ref, *, tq, ck, topk, s_eff, tile0):
    del prev_ref
    i = tile0 + pl.program_id(1)
    n_pair = N_IDX_HEADS // 2

    qi = qi_ref[...]
    qs = jnp.concatenate([qi[:, j * LANES:(j + 1) * LANES] for j in range(n_pair)], axis=0)
    w = wi_ref[...] * (N_IDX_HEADS ** -0.5 * IDX_DIM ** -0.5)
    for c in range(s_eff // ck):
        le = _dot_nt(qs, kie_ref[c * ck:(c + 1) * ck, :])
        lo = _dot_nt(qs, kio_ref[c * ck:(c + 1) * ck, :])
        acc = jnp.zeros((tq, ck), F32)
        for j in range(n_pair):
            acc = acc + jnp.maximum(le[j * tq:(j + 1) * tq], 0.0) * w[:, 2 * j:2 * j + 1]
            acc = acc + jnp.maximum(lo[j * tq:(j + 1) * tq], 0.0) * w[:, 2 * j + 1:2 * j + 2]
        score_ref[:, c * ck:(c + 1) * ck] = acc

    qpos = i * tq + lax.broadcasted_iota(jnp.int32, (tq, s_eff), 0)
    kpos = lax.broadcasted_iota(jnp.int32, (tq, s_eff), 1)
    score = jnp.where(kpos <= qpos, score_ref[...], -jnp.inf)

    def key_to_float(key):
        return pltpu.bitcast(jnp.where(key < 0, key ^ 0x7FFFFFFF, key), F32)

    def search(it, key):
        cand = key ^ lax.shift_left(jnp.int32(1), 31 - it)
        cnt = jnp.sum(jnp.where(score >= key_to_float(cand), 1.0, 0.0), axis=1, keepdims=True)
        return jnp.where(cnt >= topk, cand, key)

    key = lax.fori_loop(0, 32, search, jnp.full((tq, 1), INT_MIN, jnp.int32))
    thr = key_to_float(jnp.maximum(key, INT_MIN + 0x00800000))
    ge = score >= thr
    cnt = jnp.sum(jnp.where(ge, 1.0, 0.0), axis=1, keepdims=True)
    bias_ref[...] = jnp.where(ge, 0.0, -jnp.inf).astype(F32)

    @pl.when(jnp.max(cnt) > topk)
    def _():
        tw = 2 * LANES
        need = topk - jnp.sum(jnp.where(score > thr, 1.0, 0.0), axis=1, keepdims=True)
        tri = jnp.where(lax.broadcasted_iota(jnp.int32, (tw, tw), 0)
                        <= lax.broadcasted_iota(jnp.int32, (tw, tw), 1), 1.0, 0.0).astype(BF16)
        carry = jnp.zeros((tq, 1), F32)
        for c in range(s_eff // tw):
            kc = score[:, c * tw:(c + 1) * tw]
            eqc = jnp.where(kc == thr, 1.0, 0.0)
            rank = _dot(eqc.astype(BF16), tri) + carry
            take = (kc > thr) | ((kc == thr) & (rank <= need))
            bias_ref[:, c * tw:(c + 1) * tw] = jnp.where(take, 0.0, -jnp.inf).astype(F32)
            carry = carry + jnp.sum(eqc, axis=1, keepdims=True)

    bias = bias_ref[...]
    q = q_ref[...]
    ones = jnp.ones((s_eff, HEAD_DIM), BF16)
    for g in range(N_KV_HEADS):
        qg = jnp.concatenate(
            [q[:, (g * GROUP + h) * HEAD_DIM:(g * GROUP + h + 1) * HEAD_DIM] for h in range(GROUP)],
            axis=0)
        s = _dot_nt(qg, k_ref[:s_eff, g * HEAD_DIM:(g + 1) * HEAD_DIM])
        s = s.reshape(GROUP, tq, s_eff) + bias[None]
        p = jnp.exp2(s - jnp.max(s, axis=-1, keepdims=True)).astype(BF16)
        v_aug = jnp.concatenate([v_ref[:s_eff, g * HEAD_DIM:(g + 1) * HEAD_DIM], ones], axis=1)
        o = _dot(p.reshape(GROUP * tq, s_eff), v_aug)
        o = o[:, :HEAD_DIM] / o[:, HEAD_DIM:]
        for h in range(GROUP):
            col = (g * GROUP + h) * HEAD_DIM
            o_ref[:, col:col + HEAD_DIM] = o[h * tq:(h + 1) * tq].astype(o_ref.dtype)


def _dsa(qkv, qik, gw, *, batch, seq, ck=512):
    m = qkv.shape[0]
    topk = min(TOPK_MAX, seq // 4)
    kv_w = N_KV_HEADS * HEAD_DIM
    qi_w = N_IDX_HEADS * IDX_DIM
    y = jnp.zeros((m, D_MODEL), BF16)
    for n in range(1, seq // ck + 1):
        tq = ck if n * ck < seq else ck // 2
        nq = seq // tq
        per = ck // tq
        s_eff, tile0 = n * ck, (n - 1) * per
        row = lambda b, i, tile0=tile0, nq=nq: b * nq + tile0 + i
        y = pl.pallas_call(
            functools.partial(_dsa_kernel, tq=tq, ck=ck, topk=topk, s_eff=s_eff, tile0=tile0),
            grid=(batch, per),
            in_specs=[
                pl.BlockSpec((tq, qi_w), lambda b, i, row=row: (row(b, i), 0)),
                pl.BlockSpec((seq, LANES), lambda b, i: (b, qi_w // LANES)),
                pl.BlockSpec((seq, LANES), lambda b, i: (b, qi_w // LANES + 1)),
                pl.BlockSpec((tq, LANES), lambda b, i, row=row: (row(b, i), 2 * D_MODEL // LANES)),
                pl.BlockSpec((tq, D_MODEL), lambda b, i, row=row: (row(b, i), 0)),
                pl.BlockSpec((seq, kv_w), lambda b, i: (b, D_MODEL // kv_w)),
                pl.BlockSpec((seq, kv_w), lambda b, i: (b, D_MODEL // kv_w + 1)),
                pl.BlockSpec(memory_space=pl.ANY),
            ],
            out_specs=pl.BlockSpec((tq, D_MODEL), lambda b, i, row=row: (row(b, i), 0)),
            out_shape=jax.ShapeDtypeStruct((m, D_MODEL), BF16),
            input_output_aliases={7: 0},
            scratch_shapes=[pltpu.VMEM((tq, s_eff), F32), pltpu.VMEM((tq, s_eff), F32)],
            compiler_params=pltpu.CompilerParams(
                dimension_semantics=("parallel", "arbitrary"),
                vmem_limit_bytes=VMEM_LIMIT_BYTES),
            name=f"dsa_{s_eff}",
        )(qik, qik, qik, gw, qkv, qkv, qkv, y)
    return y


def _merge_ln_kernel(ya_ref, yr_ref, ga_ref, gr_ref, wa_ref, wr_ref, wo_ref, h_ref, g_ref, b_ref, o_ref):
    j = pl.program_id(1)

    @pl.when(j == 0)
    def _():
        o_ref[...] = jnp.zeros_like(o_ref)

    merged = (jax.nn.sigmoid(ga_ref[...]) * _dot(ya_ref[...], wa_ref[...])
              + jax.nn.sigmoid(gr_ref[...]) * _dot(yr_ref[...], wr_ref[...]))
    o_ref[...] += _dot(merged.astype(BF16), wo_ref[...])

    @pl.when(j == pl.num_programs(1) - 1)
    def _():
        y = DN_ALPHA * h_ref[...] + o_ref[...]
        o_ref[...] = _layer_norm(y, g_ref[...], b_ref[...])


def _merge_ln(y_attn, y_rnn, gw, w_attn, w_rnn, w_out, h, g, b, *, tm=512, tn=512):
    m, d = h.shape
    nn = d // tn
    return pl.pallas_call(
        _merge_ln_kernel,
        grid=(m // tm, nn),
        in_specs=[
            pl.BlockSpec((tm, y_attn.shape[1]), lambda i, j: (i, 0)),
            pl.BlockSpec((tm, y_rnn.shape[1]), lambda i, j: (i, 0)),
            pl.BlockSpec((tm, tn), lambda i, j: (i, j)),
            pl.BlockSpec((tm, tn), lambda i, j: (i, j + nn)),
            pl.BlockSpec((w_attn.shape[0], tn), lambda i, j: (0, j)),
            pl.BlockSpec((w_rnn.shape[0], tn), lambda i, j: (0, j)),
            pl.BlockSpec((tn, d), lambda i, j: (j, 0)),
            pl.BlockSpec((tm, d), lambda i, j: (i, 0)),
            pl.BlockSpec((1, d), lambda i, j: (0, 0)),
            pl.BlockSpec((1, d), lambda i, j: (0, 0)),
        ],
        out_specs=pl.BlockSpec((tm, d), lambda i, j: (i, 0)),
        out_shape=jax.ShapeDtypeStruct((m, d), F32),
        compiler_params=pltpu.CompilerParams(
            dimension_semantics=("parallel", "arbitrary"),
            vmem_limit_bytes=VMEM_LIMIT_BYTES),
        name="merge_ln",
    )(y_attn, y_rnn, gw, gw, w_attn, w_rnn, w_out, h, g, b)


def _block_diag_groups(w):
    per = LRU_BLOCKS // LRU_GROUPS
    w4 = w.reshape(LRU_GROUPS, per, LRU_BLOCK, LRU_BLOCK)
    bd = jnp.einsum("gaij,ab->gaibj", w4, jnp.eye(per, dtype=w.dtype))
    return bd.reshape(LRU_GROUPS, LRU_GROUP_W, LRU_GROUP_W)


def kernel(x, positions, ffn1_w_in, ffn1_w_out, ln1_g, ln1_b, w_in, conv_w, conv_b, lru_wa, lru_ba,
           lru_wx, lru_bx, lru_lambda, w_attn_branch, w_rnn_branch, w_out, ln2_g, ln2_b, ffn2_w_in,
           ffn2_w_out, ln3_g, ln3_b):
    batch, seq, d = x.shape
    m = batch * seq
    h = x.reshape(m, d)
    pos_col = positions.reshape(m, 1)
    tab32, tab16 = _rope_tables(pos_col)

    for l in range(DEPTH):
        h, hb = _ffn_ln(h, ffn1_w_in[l].astype(BF16), ffn1_w_out[l].astype(BF16), ln1_g[l][None],
                        ln1_b[l][None], emit_bf16=True)

        wl = w_in[l]
        o_q, o_k, o_v = 0, D_MODEL, D_MODEL + N_KV_HEADS * HEAD_DIM
        o_qi = o_v + N_KV_HEADS * HEAD_DIM
        o_ki = o_qi + N_IDX_HEADS * IDX_DIM
        o_wi = o_ki + IDX_DIM
        o_rx = o_wi + N_IDX_HEADS
        o_gt = o_rx + 2 * D_RNN
        zk = jnp.zeros((d, IDX_DIM), wl.dtype)
        w_ki = wl[:, o_ki:o_wi]
        w_qkv = wl[:, o_q:o_qi].astype(BF16)
        w_qik = jnp.concatenate([wl[:, o_qi:o_ki], w_ki, zk, zk, w_ki], axis=1).astype(BF16)
        w_rxg = wl[:, o_rx:o_gt].astype(BF16)
        pad = jnp.zeros((d, 512 - N_IDX_HEADS), wl.dtype)
        w_gw = jnp.concatenate([wl[:, o_gt:], wl[:, o_wi:o_rx], pad], axis=1).astype(BF16)

        qkv = _proj(hb, w_qkv, BF16, tm=1024, tn=512, tab=tab32, rope_shift=ROPE_DIM // 2,
                    n_rope_tiles=(D_MODEL + N_KV_HEADS * HEAD_DIM) // 512,
                    n_scale_tiles=D_MODEL // 512, scale=HEAD_DIM ** -0.5 * LOG2_E, name="proj_qkv")
        qik = _proj(hb, w_qik, BF16, tm=1024, tn=256, tab=tab16, rope_shift=IDX_ROPE_DIM // 2,
                    n_rope_tiles=w_qik.shape[1] // 256, name="proj_idx")
        rxg = _proj(hb, w_rxg, F32, tm=1024, tn=1024, name="proj_rnn")
        gw = _proj(hb, w_gw, F32, tm=1024, tn=1536, name="proj_gate")

        y_rnn = _rglru(rxg, conv_w[l], conv_b[l][None],
                       _block_diag_groups(lru_wa[l]).astype(BF16), _block_diag_groups(lru_wx[l]).astype(BF16),
                       lru_ba[l][None], lru_bx[l][None], lru_lambda[l][None], batch=batch, seq=seq)
        y_attn = _dsa(qkv, qik, gw, batch=batch, seq=seq)

        h = _merge_ln(y_attn, y_rnn, gw, w_attn_branch[l].astype(BF16), w_rnn_branch[l].astype(BF16),
                      w_out[l].astype(BF16), h, ln2_g[l][None], ln2_b[l][None])
        h = _ffn_ln(h, ffn2_w_in[l].astype(BF16), ffn2_w_out[l].astype(BF16), ln3_g[l][None], ln3_b[l][None],
                    emit_bf16=False)
    return h.reshape(batch, seq, d)
```

```python
import functools

import numpy as np
import jax
import jax.numpy as jnp
from jax import lax
from jax.experimental import pallas as pl
from jax.experimental.pallas import tpu as pltpu

F32 = jnp.float32
BF16 = jnp.bfloat16

D_MODEL = 2048
HEAD_DIM = 128
N_HEADS = D_MODEL // HEAD_DIM
N_KV_HEADS = 4
GROUP = N_HEADS // N_KV_HEADS
ROPE_DIM = HEAD_DIM // 4
ROPE_THETA = 500000.0
N_IDX_HEADS = 16
IDX_DIM = 64
IDX_ROPE_DIM = IDX_DIM // 4
TOPK_MAX = 256
D_RNN = (4 * D_MODEL // 3) // 256 * 256
LRU_BLOCKS = 16
LRU_BLOCK = D_RNN // LRU_BLOCKS
CONV_WIDTH = 4
LRU_C = 8.0
D_FF = (8 * D_MODEL // 3 + 255) // 256 * 256
LN_EPS = 1e-5
DEPTH = 1
DN_ALPHA = (2.0 * DEPTH) ** 0.25

LANES = 128
SUBLANES = 8
MXU_N = 256
VMEM_LIMIT_BYTES = 56 * 1024 * 1024

LRU_GROUPS = 4
LRU_GROUP_W = D_RNN // LRU_GROUPS

INT_MIN = -(2 ** 31)
LOG2_E = 1.4426950408889634


def _layer_norm(y, g, b):
    mu = jnp.mean(y, axis=-1, keepdims=True)
    d = y - mu
    var = jnp.mean(d * d, axis=-1, keepdims=True)
    return d * lax.rsqrt(var + LN_EPS) * g + b


def _dot(a, b):
    return jnp.dot(a, b, preferred_element_type=F32)


def _dot_nt(a, b):
    return lax.dot_general(a, b, (((1,), (1,)), ((), ())), preferred_element_type=F32)


def _ffn_ln_kernel(x_ref, wa_ref, wb_ref, wo_ref, g_ref, b_ref, o_ref, *rest):
    xb_ref = rest[-1]
    j = pl.program_id(1)

    @pl.when(j == 0)
    def _():
        xb_ref[...] = x_ref[...].astype(BF16)
        o_ref[...] = jnp.zeros_like(o_ref)

    xb = xb_ref[...]
    a = _dot(xb, wa_ref[...])
    b = _dot(xb, wb_ref[...])
    act = (jax.nn.silu(a) * b).astype(BF16)
    o_ref[...] += _dot(act, wo_ref[...])

    @pl.when(j == pl.num_programs(1) - 1)
    def _():
        y = DN_ALPHA * x_ref[...] + 0.5 * o_ref[...]
        out = _layer_norm(y, g_ref[...], b_ref[...])
        o_ref[...] = out
        if len(rest) == 2:
            rest[0][...] = out.astype(BF16)


def _ffn_ln(x, w_in, w_out, g, b, *, emit_bf16, tm=512, tf=512):
    m, d = x.shape
    f = w_out.shape[0]
    nf = f // tf
    out_spec = pl.BlockSpec((tm, d), lambda i, j: (i, 0))
    out_specs, out_shape = out_spec, jax.ShapeDtypeStruct((m, d), F32)
    if emit_bf16:
        out_specs, out_shape = [out_spec, out_spec], [out_shape, jax.ShapeDtypeStruct((m, d), BF16)]
    return pl.pallas_call(
        _ffn_ln_kernel,
        grid=(m // tm, nf),
        in_specs=[
            pl.BlockSpec((tm, d), lambda i, j: (i, 0)),
            pl.BlockSpec((d, tf), lambda i, j: (0, j)),
            pl.BlockSpec((d, tf), lambda i, j: (0, j + nf)),
            pl.BlockSpec((tf, d), lambda i, j: (j, 0)),
            pl.BlockSpec((1, d), lambda i, j: (0, 0)),
            pl.BlockSpec((1, d), lambda i, j: (0, 0)),
        ],
        out_specs=out_specs,
        out_shape=out_shape,
        scratch_shapes=[pltpu.VMEM((tm, d), BF16)],
        compiler_params=pltpu.CompilerParams(
            dimension_semantics=("parallel", "arbitrary"),
            vmem_limit_bytes=VMEM_LIMIT_BYTES),
        name="ffn_ln",
    )(x, w_in, w_in, w_out, g, b)


def _rope_tab_kernel(pos_ref, f_ref, o32_ref, o16_ref):
    pos = pos_ref[...].astype(F32)
    for row, o_ref in ((0, o32_ref), (3, o16_ref)):
        ang = pos * f_ref[row:row + 1, :]
        s = jnp.sin(ang)
        o_ref[0] = jnp.cos(ang)
        o_ref[1] = s * f_ref[row + 1:row + 2, :]
        o_ref[2] = s * f_ref[row + 2:row + 3, :]


def _rope_lane_table():
    lane = np.arange(LANES)

    def rows(rot_dim, period):
        half = rot_dim // 2
        inv_freq = jnp.power(ROPE_THETA, -(jnp.arange(half, dtype=F32) * 2.0 / rot_dim))
        l = lane % period
        freq = jnp.where(l < rot_dim, inv_freq[l % half], 0.0)
        plus = ((l >= half) & (l < rot_dim)).astype(np.float32)
        minus = -(l < half).astype(np.float32)
        return [freq, jnp.asarray(plus), jnp.asarray(minus)]

    z = jnp.zeros((LANES,), F32)
    return jnp.stack(rows(ROPE_DIM, HEAD_DIM) + rows(IDX_ROPE_DIM, IDX_DIM) + [z, z]).astype(F32)


def _rope_tables(pos_col, *, tm=1024):
    m = pos_col.shape[0]
    shp = jax.ShapeDtypeStruct((3, m, LANES), F32)
    return pl.pallas_call(
        _rope_tab_kernel,
        grid=(m // tm,),
        in_specs=[pl.BlockSpec((tm, 1), lambda i: (i, 0)),
                  pl.BlockSpec((8, LANES), lambda i: (0, 0))],
        out_specs=[pl.BlockSpec((3, tm, LANES), lambda i: (0, i, 0)),
                   pl.BlockSpec((3, tm, LANES), lambda i: (0, i, 0))],
        out_shape=[shp, shp],
        compiler_params=pltpu.CompilerParams(dimension_semantics=("parallel",)),
        name="rope_tables",
    )(pos_col, _rope_lane_table())


def _proj_kernel(x_ref, w_ref, *rest, tn, rope_shift, n_rope_tiles, n_scale_tiles, scale):
    if rope_shift is None:
        (o_ref,) = rest
        o_ref[...] = _dot(x_ref[...], w_ref[...]).astype(o_ref.dtype)
        return
    tab_ref, o_ref = rest
    j = pl.program_id(1)
    roped = j < n_rope_tiles
    sc = jnp.where(j < n_scale_tiles, scale, 1.0).astype(F32)
    c = jnp.where(roped, tab_ref[0], 1.0) * sc
    s_hi = jnp.where(roped, tab_ref[1], 0.0) * sc
    s_lo = jnp.where(roped, tab_ref[2], 0.0) * sc
    x = x_ref[...]
    for u in range(tn // MXU_N):
        acc = _dot(x, w_ref[:, u * MXU_N:(u + 1) * MXU_N])
        for t in range(MXU_N // LANES):
            a = acc[:, t * LANES:(t + 1) * LANES]
            r = (a * c + pltpu.roll(a, rope_shift, 1) * s_hi
                 + pltpu.roll(a, LANES - rope_shift, 1) * s_lo)
            col = u * MXU_N + t * LANES
            o_ref[:, col:col + LANES] = r.astype(o_ref.dtype)


def _proj(xb, w, out_dtype, *, tm, tn, tab=None, rope_shift=None, n_rope_tiles=0,
          n_scale_tiles=0, scale=1.0, name="proj"):
    m, d = xb.shape
    n = w.shape[1]
    in_specs = [pl.BlockSpec((tm, d), lambda i, j: (i, 0)),
                pl.BlockSpec((d, tn), lambda i, j: (0, j))]
    args = [xb, w]
    if rope_shift is not None:
        in_specs.append(pl.BlockSpec((3, tm, LANES), lambda i, j: (0, i, 0)))
        args.append(tab)
    return pl.pallas_call(
        functools.partial(_proj_kernel, tn=tn, rope_shift=rope_shift, n_rope_tiles=n_rope_tiles,
                          n_scale_tiles=n_scale_tiles, scale=scale),
        grid=(m // tm, n // tn),
        in_specs=in_specs,
        out_specs=pl.BlockSpec((tm, tn), lambda i, j: (i, j)),
        out_shape=jax.ShapeDtypeStruct((m, n), out_dtype),
        compiler_params=pltpu.CompilerParams(
            dimension_semantics=("parallel", "arbitrary"),
            vmem_limit_bytes=VMEM_LIMIT_BYTES),
        name=name,
    )(*args)


def _rglru_kernel(rx_ref, rg_ref, cw_ref, cb_ref, wa_ref, wx_ref, ba_ref, bx_ref, lam_ref,
                  o_ref, xbuf, a_s, b_s, h_s, *, tt):
    t = pl.program_id(2)
    w = xbuf.shape[1]

    @pl.when(t == 0)
    def _():
        xbuf[0:SUBLANES, :] = jnp.zeros((SUBLANES, w), F32)
        h_s[...] = jnp.zeros_like(h_s)

    xbuf[SUBLANES:SUBLANES + tt, :] = rx_ref[...]
    xc = cb_ref[...]
    for k in range(CONV_WIDTH):
        off = SUBLANES - (CONV_WIDTH - 1) + k
        xc = xc + cw_ref[k:k + 1, :] * xbuf[off:off + tt, :]
    xbuf[0:SUBLANES, :] = rx_ref[tt - SUBLANES:tt, :]

    xcb = xc.astype(BF16)
    r = jax.nn.sigmoid(_dot(xcb, wa_ref[0]) + ba_ref[...])
    ig = jax.nn.sigmoid(_dot(xcb, wx_ref[0]) + bx_ref[...])
    log_a = -LRU_C * r * jax.nn.softplus(-lam_ref[...])
    a = jnp.exp(log_a)
    a_s[...] = a
    b_s[...] = jnp.sqrt(-jnp.tanh(log_a) * (a * a + 1.0)) * (ig * xc)

    row = lax.broadcasted_iota(jnp.int32, (SUBLANES, w), 0)

    def body(g, h):
        sl = pl.ds(pl.multiple_of(g * SUBLANES, SUBLANES), SUBLANES)
        a8 = a_s[sl, :]
        b8 = b_s[sl, :]
        for d in (1, 2, 4):
            keep = row >= d
            a_sh = jnp.where(keep, pltpu.roll(a8, d, 0), 1.0)
            b_sh = jnp.where(keep, pltpu.roll(b8, d, 0), 0.0)
            b8 = a8 * b_sh + b8
            a8 = a8 * a_sh
        hrows = a8 * h + b8
        b_s[sl, :] = hrows
        return jnp.broadcast_to(hrows[SUBLANES - 1:SUBLANES, :], (SUBLANES, w))

    h_s[...] = lax.fori_loop(0, tt // SUBLANES, body, h_s[...])
    o_ref[...] = (b_s[...] * jax.nn.gelu(rg_ref[...])).astype(o_ref.dtype)


def _rglru(rxg, conv_w, conv_b, wa_bd, wx_bd, ba, bx, lam, *, batch, seq, tt=512):
    m = rxg.shape[0]
    nt = seq // tt
    gw = LRU_GROUP_W
    vec = lambda: pl.BlockSpec((1, gw), lambda b, g, t: (0, g))
    return pl.pallas_call(
        functools.partial(_rglru_kernel, tt=tt),
        grid=(batch, LRU_GROUPS, nt),
        in_specs=[
            pl.BlockSpec((tt, gw), lambda b, g, t: (b * nt + t, g)),
            pl.BlockSpec((tt, gw), lambda b, g, t: (b * nt + t, g + LRU_GROUPS)),
            pl.BlockSpec((CONV_WIDTH, gw), lambda b, g, t: (0, g)),
            vec(),
            pl.BlockSpec((1, gw, gw), lambda b, g, t: (g, 0, 0)),
            pl.BlockSpec((1, gw, gw), lambda b, g, t: (g, 0, 0)),
            vec(), vec(), vec(),
        ],
        out_specs=pl.BlockSpec((tt, gw), lambda b, g, t: (b * nt + t, g)),
        out_shape=jax.ShapeDtypeStruct((m, D_RNN), BF16),
        scratch_shapes=[pltpu.VMEM((tt + SUBLANES, gw), F32), pltpu.VMEM((tt, gw), F32),
                        pltpu.VMEM((tt, gw), F32), pltpu.VMEM((SUBLANES, gw), F32)],
        compiler_params=pltpu.CompilerParams(
            dimension_semantics=("parallel", "parallel", "arbitrary"),
            vmem_limit_bytes=VMEM_LIMIT_BYTES),
        name="rglru",
    )(rxg, rxg, conv_w, conv_b, wa_bd, wx_bd, ba, bx, lam)


def _dsa_kernel(qi_ref, kie_ref, kio_ref, wi_ref, q_ref, k_ref, v_ref, prev_ref, o_ref,
                score_ref, bias_ref, *, tq, ck, topk, s_eff, tile0):
    del prev_ref
    i = tile0 + pl.program_id(1)
    n_pair = N_IDX_HEADS // 2

    qi = qi_ref[...]
    qs = jnp.concatenate([qi[:, j * LANES:(j + 1) * LANES] for j in range(n_pair)], axis=0)
    w = wi_ref[...] * (N_IDX_HEADS ** -0.5 * IDX_DIM ** -0.5)
    for c in range(s_eff // ck):
        le = _dot_nt(qs, kie_ref[c * ck:(c + 1) * ck, :])
        lo = _dot_nt(qs, kio_ref[c * ck:(c + 1) * ck, :])
        acc = jnp.zeros((tq, ck), F32)
        for j in range(n_pair):
            acc = acc + jnp.maximum(le[j * tq:(j + 1) * tq], 0.0) * w[:, 2 * j:2 * j + 1]
            acc = acc + jnp.maximum(lo[j * tq:(j + 1) * tq], 0.0) * w[:, 2 * j + 1:2 * j + 2]
        score_ref[:, c * ck:(c + 1) * ck] = acc

    qpos = i * tq + lax.broadcasted_iota(jnp.int32, (tq, s_eff), 0)
    kpos = lax.broadcasted_iota(jnp.int32, (tq, s_eff), 1)
    score = jnp.where(kpos <= qpos, score_ref[...], -jnp.inf)

    def key_to_float(key):
        return pltpu.bitcast(jnp.where(key < 0, key ^ 0x7FFFFFFF, key), F32)

    def search(it, key):
        cand = key ^ lax.shift_left(jnp.int32(1), 31 - it)
        cnt = jnp.sum(jnp.where(score >= key_to_float(cand), 1.0, 0.0), axis=1, keepdims=True)
        return jnp.where(cnt >= topk, cand, key)

    key = lax.fori_loop(0, 32, search, jnp.full((tq, 1), INT_MIN, jnp.int32))
    thr = key_to_float(jnp.maximum(key, INT_MIN + 0x00800000))
    ge = score >= thr
    cnt = jnp.sum(jnp.where(ge, 1.0, 0.0), axis=1, keepdims=True)
    bias_ref[...] = jnp.where(ge, 0.0, -jnp.inf).astype(F32)

    @pl.when(jnp.max(cnt) > topk)
    def _():
        tw = 2 * LANES
        need = topk - jnp.sum(jnp.where(score > thr, 1.0, 0.0), axis=1, keepdims=True)
        tri = jnp.where(lax.broadcasted_iota(jnp.int32, (tw, tw), 0)
                        <= lax.broadcasted_iota(jnp.int32, (tw, tw), 1), 1.0, 0.0).astype(BF16)
        carry = jnp.zeros((tq, 1), F32)
        for c in range(s_eff // tw):
            kc = score[:, c * tw:(c + 1) * tw]
            eqc = jnp.where(kc == thr, 1.0, 0.0)
            rank = _dot(eqc.astype(BF16), tri) + carry
            take = (kc > thr) | ((kc == thr) & (rank <= need))
            bias_ref[:, c * tw:(c + 1) * tw] = jnp.where(take, 0.0, -jnp.inf).astype(F32)
            carry = carry + jnp.sum(eqc, axis=1, keepdims=True)

    bias = bias_ref[...]
    q = q_ref[...]
    ones = jnp.ones((s_eff, HEAD_DIM), BF16)
    for g in range(N_KV_HEADS):
        qg = jnp.concatenate(
            [q[:, (g * GROUP + h) * HEAD_DIM:(g * GROUP + h + 1) * HEAD_DIM] for h in range(GROUP)],
            axis=0)
        s = _dot_nt(qg, k_ref[:s_eff, g * HEAD_DIM:(g + 1) * HEAD_DIM])
        s = s.reshape(GROUP, tq, s_eff) + bias[None]
        p = jnp.exp2(s - jnp.max(s, axis=-1, keepdims=True)).astype(BF16)
        v_aug = jnp.concatenate([v_ref[:s_eff, g * HEAD_DIM:(g + 1) * HEAD_DIM], ones], axis=1)
        o = _dot(p.reshape(GROUP * tq, s_eff), v_aug)
        o = o[:, :HEAD_DIM] / o[:, HEAD_DIM:]
        for h in range(GROUP):
            col = (g * GROUP + h) * HEAD_DIM
            o_ref[:, col:col + HEAD_DIM] = o[h * tq:(h + 1) * tq].astype(o_ref.dtype)


def _dsa(qkv, qik, gw, *, batch, seq, ck=512):
    m = qkv.shape[0]
    topk = min(TOPK_MAX, seq // 4)
    kv_w = N_KV_HEADS * HEAD_DIM
    qi_w = N_IDX_HEADS * IDX_DIM
    y = jnp.zeros((m, D_MODEL), BF16)
    for n in range(1, seq // ck + 1):
        tq = ck if 2 * n * ck <= seq else ck // 2
        nq = seq // tq
        per = ck // tq
        s_eff, tile0 = n * ck, (n - 1) * per
        row = lambda b, i, tile0=tile0, nq=nq: b * nq + tile0 + i
        y = pl.pallas_call(
            functools.partial(_dsa_kernel, tq=tq, ck=ck, topk=topk, s_eff=s_eff, tile0=tile0),
            grid=(batch, per),
            in_specs=[
                pl.BlockSpec((tq, qi_w), lambda b, i, row=row: (row(b, i), 0)),
                pl.BlockSpec((seq, LANES), lambda b, i: (b, qi_w // LANES)),
                pl.BlockSpec((seq, LANES), lambda b, i: (b, qi_w // LANES + 1)),
                pl.BlockSpec((tq, LANES), lambda b, i, row=row: (row(b, i), 2 * D_MODEL // LANES)),
                pl.BlockSpec((tq, D_MODEL), lambda b, i, row=row: (row(b, i), 0)),
                pl.BlockSpec((seq, kv_w), lambda b, i: (b, D_MODEL // kv_w)),
                pl.BlockSpec((seq, kv_w), lambda b, i: (b, D_MODEL // kv_w + 1)),
                pl.BlockSpec(memory_space=pl.ANY),
            ],
            out_specs=pl.BlockSpec((tq, D_MODEL), lambda b, i, row=row: (row(b, i), 0)),
            out_shape=jax.ShapeDtypeStruct((m, D_MODEL), BF16),
            input_output_aliases={7: 0},
            scratch_shapes=[pltpu.VMEM((tq, s_eff), F32), pltpu.VMEM((tq, s_eff), F32)],
            compiler_params=pltpu.CompilerParams(
                dimension_semantics=("parallel", "arbitrary"),
                vmem_limit_bytes=VMEM_LIMIT_BYTES),
            name=f"dsa_{s_eff}",
        )(qik, qik, qik, gw, qkv, qkv, qkv, y)
    return y


def _merge_ln_kernel(ya_ref, yr_ref, ga_ref, gr_ref, wa_ref, wr_ref, wo_ref, h_ref, g_ref, b_ref, o_ref):
    j = pl.program_id(1)

    @pl.when(j == 0)
    def _():
        o_ref[...] = jnp.zeros_like(o_ref)

    merged = (jax.nn.sigmoid(ga_ref[...]) * _dot(ya_ref[...], wa_ref[...])
              + jax.nn.sigmoid(gr_ref[...]) * _dot(yr_ref[...], wr_ref[...]))
    o_ref[...] += _dot(merged.astype(BF16), wo_ref[...])

    @pl.when(j == pl.num_programs(1) - 1)
    def _():
        y = DN_ALPHA * h_ref[...] + o_ref[...]
        o_ref[...] = _layer_norm(y, g_ref[...], b_ref[...])


def _merge_ln(y_attn, y_rnn, gw, w_attn, w_rnn, w_out, h, g, b, *, tm=512, tn=512):
    m, d = h.shape
    nn = d // tn
    return pl.pallas_call(
        _merge_ln_kernel,
        grid=(m // tm, nn),
        in_specs=[
            pl.BlockSpec((tm, y_attn.shape[1]), lambda i, j: (i, 0)),
            pl.BlockSpec((tm, y_rnn.shape[1]), lambda i, j: (i, 0)),
            pl.BlockSpec((tm, tn), lambda i, j: (i, j)),
            pl.BlockSpec((tm, tn), lambda i, j: (i, j + nn)),
            pl.BlockSpec((w_attn.shape[0], tn), lambda i, j: (0, j)),
            pl.BlockSpec((w_rnn.shape[0], tn), lambda i, j: (0, j)),
            pl.BlockSpec((tn, d), lambda i, j: (j, 0)),
            pl.BlockSpec((tm, d), lambda i, j: (i, 0)),
            pl.BlockSpec((1, d), lambda i, j: (0, 0)),
            pl.BlockSpec((1, d), lambda i, j: (0, 0)),
        ],
        out_specs=pl.BlockSpec((tm, d), lambda i, j: (i, 0)),
        out_shape=jax.ShapeDtypeStruct((m, d), F32),
        compiler_params=pltpu.CompilerParams(
            dimension_semantics=("parallel", "arbitrary"),
            vmem_limit_bytes=VMEM_LIMIT_BYTES),
        name="merge_ln",
    )(y_attn, y_rnn, gw, gw, w_attn, w_rnn, w_out, h, g, b)


def _block_diag_groups(w):
    per = LRU_BLOCKS // LRU_GROUPS
    w4 = w.reshape(LRU_GROUPS, per, LRU_BLOCK, LRU_BLOCK)
    bd = jnp.einsum("gaij,ab->gaibj", w4, jnp.eye(per, dtype=w.dtype))
    return bd.reshape(LRU_GROUPS, LRU_GROUP_W, LRU_GROUP_W)


def kernel(x, positions, ffn1_w_in, ffn1_w_out, ln1_g, ln1_b, w_in, conv_w, conv_b, lru_wa, lru_ba,
           lru_wx, lru_bx, lru_lambda, w_attn_branch, w_rnn_branch, w_out, ln2_g, ln2_b, ffn2_w_in,
           ffn2_w_out, ln3_g, ln3_b):
    batch, seq, d = x.shape
    m = batch * seq
    h = x.reshape(m, d)
    pos_col = positions.reshape(m, 1)
    tab32, tab16 = _rope_tables(pos_col)

    for l in range(DEPTH):
        h, hb = _ffn_ln(h, ffn1_w_in[l].astype(BF16), ffn1_w_out[l].astype(BF16), ln1_g[l][None],
                        ln1_b[l][None], emit_bf16=True)

        wl = w_in[l]
        o_q, o_k, o_v = 0, D_MODEL, D_MODEL + N_KV_HEADS * HEAD_DIM
        o_qi = o_v + N_KV_HEADS * HEAD_DIM
        o_ki = o_qi + N_IDX_HEADS * IDX_DIM
        o_wi = o_ki + IDX_DIM
        o_rx = o_wi + N_IDX_HEADS
        o_gt = o_rx + 2 * D_RNN
        zk = jnp.zeros((d, IDX_DIM), wl.dtype)
        w_ki = wl[:, o_ki:o_wi]
        w_qkv = wl[:, o_q:o_qi].astype(BF16)
        w_qik = jnp.concatenate([wl[:, o_qi:o_ki], w_ki, zk, zk, w_ki], axis=1).astype(BF16)
        w_rxg = wl[:, o_rx:o_gt].astype(BF16)
        pad = jnp.zeros((d, 512 - N_IDX_HEADS), wl.dtype)
        w_gw = jnp.concatenate([wl[:, o_gt:], wl[:, o_wi:o_rx], pad], axis=1).astype(BF16)

        qkv = _proj(hb, w_qkv, BF16, tm=1024, tn=512, tab=tab32, rope_shift=ROPE_DIM // 2,
                    n_rope_tiles=(D_MODEL + N_KV_HEADS * HEAD_DIM) // 512,
                    n_scale_tiles=D_MODEL // 512, scale=HEAD_DIM ** -0.5 * LOG2_E, name="proj_qkv")
        qik = _proj(hb, w_qik, BF16, tm=1024, tn=w_qik.shape[1], tab=tab16, rope_shift=IDX_ROPE_DIM // 2,
                    n_rope_tiles=1, name="proj_idx")
        rxg = _proj(hb, w_rxg, F32, tm=1024, tn=1024, name="proj_rnn")
        gw = _proj(hb, w_gw, F32, tm=1024, tn=1536, name="proj_gate")

        y_rnn = _rglru(rxg, conv_w[l], conv_b[l][None],
                       _block_diag_groups(lru_wa[l]).astype(BF16), _block_diag_groups(lru_wx[l]).astype(BF16),
                       lru_ba[l][None], lru_bx[l][None], lru_lambda[l][None], batch=batch, seq=seq)
        y_attn = _dsa(qkv, qik, gw, batch=batch, seq=seq)

        h = _merge_ln(y_attn, y_rnn, gw, w_attn_branch[l].astype(BF16), w_rnn_branch[l].astype(BF16),
                      w_out[l].astype(BF16), h, ln2_g[l][None], ln2_b[l][None])
        h = _ffn_ln(h, ffn2_w_in[l].astype(BF16), ffn2_w_out[l].astype(BF16), ln3_g[l][None], ln3_b[l][None],
                    emit_bf16=False)
    return h.reshape(batch, seq, d)
```

```python
import functools

import numpy as np
import jax
import jax.numpy as jnp
from jax import lax
from jax.experimental import pallas as pl
from jax.experimental.pallas import tpu as pltpu

F32 = jnp.float32
BF16 = jnp.bfloat16

D_MODEL = 2048
HEAD_DIM = 128
N_HEADS = D_MODEL // HEAD_DIM
N_KV_HEADS = 4
GROUP = N_HEADS // N_KV_HEADS
ROPE_DIM = HEAD_DIM // 4
ROPE_THETA = 500000.0
N_IDX_HEADS = 16
IDX_DIM = 64
IDX_ROPE_DIM = IDX_DIM // 4
TOPK_MAX = 256
D_RNN = (4 * D_MODEL // 3) // 256 * 256
LRU_BLOCKS = 16
LRU_BLOCK = D_RNN // LRU_BLOCKS
CONV_WIDTH = 4
LRU_C = 8.0
D_FF = (8 * D_MODEL // 3 + 255) // 256 * 256
LN_EPS = 1e-5
DEPTH = 1
DN_ALPHA = (2.0 * DEPTH) ** 0.25

LANES = 128
SUBLANES = 8
MXU_N = 256
VMEM_LIMIT_BYTES = 56 * 1024 * 1024

LRU_GROUPS = 4
LRU_GROUP_W = D_RNN // LRU_GROUPS

INT_MIN = -(2 ** 31)
LOG2_E = 1.4426950408889634


def _layer_norm(y, g, b):
    mu = jnp.mean(y, axis=-1, keepdims=True)
    d = y - mu
    var = jnp.mean(d * d, axis=-1, keepdims=True)
    return d * lax.rsqrt(var + LN_EPS) * g + b


def _dot(a, b):
    return jnp.dot(a, b, preferred_element_type=F32)


def _dot_nt(a, b):
    return lax.dot_general(a, b, (((1,), (1,)), ((), ())), preferred_element_type=F32)


def _ffn_ln_kernel(x_ref, wa_ref, wb_ref, wo_ref, g_ref, b_ref, o_ref, *rest):
    xb_ref = rest[-1]
    j = pl.program_id(1)

    @pl.when(j == 0)
    def _():
        xb_ref[...] = x_ref[...].astype(BF16)
        o_ref[...] = jnp.zeros_like(o_ref)

    xb = xb_ref[...]
    a = _dot(xb, wa_ref[...])
    b = _dot(xb, wb_ref[...])
    act = (jax.nn.silu(a) * b).astype(BF16)
    o_ref[...] += _dot(act, wo_ref[...])

    @pl.when(j == pl.num_programs(1) - 1)
    def _():
        y = DN_ALPHA * x_ref[...] + 0.5 * o_ref[...]
        out = _layer_norm(y, g_ref[...], b_ref[...])
        o_ref[...] = out
        if len(rest) == 2:
            rest[0][...] = out.astype(BF16)


def _ffn_ln(x, w_in, w_out, g, b, *, emit_bf16, tm=512, tf=512):
    m, d = x.shape
    f = w_out.shape[0]
    nf = f // tf
    out_spec = pl.BlockSpec((tm, d), lambda i, j: (i, 0))
    out_specs, out_shape = out_spec, jax.ShapeDtypeStruct((m, d), F32)
    if emit_bf16:
        out_specs, out_shape = [out_spec, out_spec], [out_shape, jax.ShapeDtypeStruct((m, d), BF16)]
    return pl.pallas_call(
        _ffn_ln_kernel,
        grid=(m // tm, nf),
        in_specs=[
            pl.BlockSpec((tm, d), lambda i, j: (i, 0)),
            pl.BlockSpec((d, tf), lambda i, j: (0, j)),
            pl.BlockSpec((d, tf), lambda i, j: (0, j + nf)),
            pl.BlockSpec((tf, d), lambda i, j: (j, 0)),
            pl.BlockSpec((1, d), lambda i, j: (0, 0)),
            pl.BlockSpec((1, d), lambda i, j: (0, 0)),
        ],
        out_specs=out_specs,
        out_shape=out_shape,
        scratch_shapes=[pltpu.VMEM((tm, d), BF16)],
        compiler_params=pltpu.CompilerParams(
            dimension_semantics=("parallel", "arbitrary"),
            vmem_limit_bytes=VMEM_LIMIT_BYTES),
        name="ffn_ln",
    )(x, w_in, w_in, w_out, g, b)


def _rope_tab_kernel(pos_ref, f_ref, o32_ref, o16_ref):
    pos = pos_ref[...].astype(F32)
    for row, o_ref in ((0, o32_ref), (3, o16_ref)):
        ang = pos * f_ref[row:row + 1, :]
        s = jnp.sin(ang)
        o_ref[0] = jnp.cos(ang)
        o_ref[1] = s * f_ref[row + 1:row + 2, :]
        o_ref[2] = s * f_ref[row + 2:row + 3, :]


def _rope_lane_table():
    lane = np.arange(LANES)

    def rows(rot_dim, period):
        half = rot_dim // 2
        inv_freq = jnp.power(ROPE_THETA, -(jnp.arange(half, dtype=F32) * 2.0 / rot_dim))
        l = lane % period
        freq = jnp.where(l < rot_dim, inv_freq[l % half], 0.0)
        plus = ((l >= half) & (l < rot_dim)).astype(np.float32)
        minus = -(l < half).astype(np.float32)
        return [freq, jnp.asarray(plus), jnp.asarray(minus)]

    z = jnp.zeros((LANES,), F32)
    return jnp.stack(rows(ROPE_DIM, HEAD_DIM) + rows(IDX_ROPE_DIM, IDX_DIM) + [z, z]).astype(F32)


def _rope_tables(pos_col, *, tm=1024):
    m = pos_col.shape[0]
    shp = jax.ShapeDtypeStruct((3, m, LANES), F32)
    return pl.pallas_call(
        _rope_tab_kernel,
        grid=(m // tm,),
        in_specs=[pl.BlockSpec((tm, 1), lambda i: (i, 0)),
                  pl.BlockSpec((8, LANES), lambda i: (0, 0))],
        out_specs=[pl.BlockSpec((3, tm, LANES), lambda i: (0, i, 0)),
                   pl.BlockSpec((3, tm, LANES), lambda i: (0, i, 0))],
        out_shape=[shp, shp],
        compiler_params=pltpu.CompilerParams(dimension_semantics=("parallel",)),
        name="rope_tables",
    )(pos_col, _rope_lane_table())


def _proj_kernel(x_ref, w_ref, *rest, tn, rope_shift, n_rope_tiles, n_scale_tiles, scale):
    if rope_shift is None:
        (o_ref,) = rest
        o_ref[...] = _dot(x_ref[...], w_ref[...]).astype(o_ref.dtype)
        return
    tab_ref, o_ref = rest
    plain = (tab_ref[0], tab_ref[1], tab_ref[2])
    scaled = tuple(t * scale for t in plain)
    x = x_ref[...]
    for u in range(tn // MXU_N):
        acc = _dot(x, w_ref[:, u * MXU_N:(u + 1) * MXU_N])
        for t in range(MXU_N // LANES):
            chunk = u * (MXU_N // LANES) + t
            a = acc[:, t * LANES:(t + 1) * LANES]
            if chunk < n_rope_tiles:
                c, s_hi, s_lo = scaled if chunk < n_scale_tiles else plain
                a = (a * c + pltpu.roll(a, rope_shift, 1) * s_hi
                     + pltpu.roll(a, LANES - rope_shift, 1) * s_lo)
            o_ref[:, chunk * LANES:(chunk + 1) * LANES] = a.astype(o_ref.dtype)


def _proj(xb, w, out_dtype, *, tm, tn, tab=None, rope_shift=None, n_rope_tiles=0,
          n_scale_tiles=0, scale=1.0, name="proj"):
    m, d = xb.shape
    n = w.shape[1]
    in_specs = [pl.BlockSpec((tm, d), lambda i, j: (i, 0)),
                pl.BlockSpec((d, tn), lambda i, j: (0, j))]
    args = [xb, w]
    if rope_shift is not None:
        assert tn == n, "the rotary epilogue numbers its 128-lane chunks from the start of the group"
        in_specs.append(pl.BlockSpec((3, tm, LANES), lambda i, j: (0, i, 0)))
        args.append(tab)
    return pl.pallas_call(
        functools.partial(_proj_kernel, tn=tn, rope_shift=rope_shift, n_rope_tiles=n_rope_tiles,
                          n_scale_tiles=n_scale_tiles, scale=scale),
        grid=(m // tm, n // tn),
        in_specs=in_specs,
        out_specs=pl.BlockSpec((tm, tn), lambda i, j: (i, j)),
        out_shape=jax.ShapeDtypeStruct((m, n), out_dtype),
        compiler_params=pltpu.CompilerParams(
            dimension_semantics=("parallel", "arbitrary"),
            vmem_limit_bytes=VMEM_LIMIT_BYTES),
        name=name,
    )(*args)


def _rglru_kernel(rx_ref, rg_ref, cw_ref, cb_ref, wa_ref, wx_ref, ba_ref, bx_ref, lam_ref,
                  o_ref, xbuf, a_s, b_s, h_s, *, tt):
    t = pl.program_id(2)
    w = xbuf.shape[1]

    @pl.when(t == 0)
    def _():
        xbuf[0:SUBLANES, :] = jnp.zeros((SUBLANES, w), F32)
        h_s[...] = jnp.zeros_like(h_s)

    xbuf[SUBLANES:SUBLANES + tt, :] = rx_ref[...]
    xc = cb_ref[...]
    for k in range(CONV_WIDTH):
        off = SUBLANES - (CONV_WIDTH - 1) + k
        xc = xc + cw_ref[k:k + 1, :] * xbuf[off:off + tt, :]
    xbuf[0:SUBLANES, :] = rx_ref[tt - SUBLANES:tt, :]

    xcb = xc.astype(BF16)
    r = jax.nn.sigmoid(_dot(xcb, wa_ref[0]) + ba_ref[...])
    ig = jax.nn.sigmoid(_dot(xcb, wx_ref[0]) + bx_ref[...])
    log_a = -LRU_C * r * jax.nn.softplus(-lam_ref[...])
    a = jnp.exp(log_a)
    a_s[...] = a
    b_s[...] = jnp.sqrt(-jnp.tanh(log_a) * (a * a + 1.0)) * (ig * xc)

    row = lax.broadcasted_iota(jnp.int32, (SUBLANES, w), 0)

    def body(g, h):
        sl = pl.ds(pl.multiple_of(g * SUBLANES, SUBLANES), SUBLANES)
        a8 = a_s[sl, :]
        b8 = b_s[sl, :]
        for d in (1, 2, 4):
            keep = row >= d
            a_sh = jnp.where(keep, pltpu.roll(a8, d, 0), 1.0)
            b_sh = jnp.where(keep, pltpu.roll(b8, d, 0), 0.0)
            b8 = a8 * b_sh + b8
            a8 = a8 * a_sh
        hrows = a8 * h + b8
        b_s[sl, :] = hrows
        return jnp.broadcast_to(hrows[SUBLANES - 1:SUBLANES, :], (SUBLANES, w))

    h_s[...] = lax.fori_loop(0, tt // SUBLANES, body, h_s[...])
    o_ref[...] = (b_s[...] * jax.nn.gelu(rg_ref[...])).astype(o_ref.dtype)


def _rglru(rxg, conv_w, conv_b, wa_bd, wx_bd, ba, bx, lam, *, batch, seq, tt=512):
    m = rxg.shape[0]
    nt = seq // tt
    gw = LRU_GROUP_W
    vec = lambda: pl.BlockSpec((1, gw), lambda b, g, t: (0, g))
    return pl.pallas_call(
        functools.partial(_rglru_kernel, tt=tt),
        grid=(batch, LRU_GROUPS, nt),
        in_specs=[
            pl.BlockSpec((tt, gw), lambda b, g, t: (b * nt + t, g)),
            pl.BlockSpec((tt, gw), lambda b, g, t: (b * nt + t, g + LRU_GROUPS)),
            pl.BlockSpec((CONV_WIDTH, gw), lambda b, g, t: (0, g)),
            vec(),
            pl.BlockSpec((1, gw, gw), lambda b, g, t: (g, 0, 0)),
            pl.BlockSpec((1, gw, gw), lambda b, g, t: (g, 0, 0)),
            vec(), vec(), vec(),
        ],
        out_specs=pl.BlockSpec((tt, gw), lambda b, g, t: (b * nt + t, g)),
        out_shape=jax.ShapeDtypeStruct((m, D_RNN), BF16),
        scratch_shapes=[pltpu.VMEM((tt + SUBLANES, gw), F32), pltpu.VMEM((tt, gw), F32),
                        pltpu.VMEM((tt, gw), F32), pltpu.VMEM((SUBLANES, gw), F32)],
        compiler_params=pltpu.CompilerParams(
            dimension_semantics=("parallel", "parallel", "arbitrary"),
            vmem_limit_bytes=VMEM_LIMIT_BYTES),
        name="rglru",
    )(rxg, rxg, conv_w, conv_b, wa_bd, wx_bd, ba, bx, lam)


def _dsa_kernel(qi_ref, kie_ref, kio_ref, wi_ref, q_ref, k_ref, v_ref, prev_ref, o_ref,
                score_ref, bias_ref, *, tq, ck, topk, s_eff, tile0):
    del prev_ref
    i = tile0 + pl.program_id(1)
    n_pair = N_IDX_HEADS // 2

    qi = qi_ref[...]
    qs = jnp.concatenate([qi[:, j * LANES:(j + 1) * LANES] for j in range(n_pair)], axis=0)
    w = wi_ref[...] * (N_IDX_HEADS ** -0.5 * IDX_DIM ** -0.5)
    for c in range(s_eff // ck):
        le = _dot_nt(qs, kie_ref[c * ck:(c + 1) * ck, :])
        lo = _dot_nt(qs, kio_ref[c * ck:(c + 1) * ck, :])
        acc = jnp.zeros((tq, ck), F32)
        for j in range(n_pair):
            acc = acc + jnp.maximum(le[j * tq:(j + 1) * tq], 0.0) * w[:, 2 * j:2 * j + 1]
            acc = acc + jnp.maximum(lo[j * tq:(j + 1) * tq], 0.0) * w[:, 2 * j + 1:2 * j + 2]
        score_ref[:, c * ck:(c + 1) * ck] = acc

    qpos = i * tq + lax.broadcasted_iota(jnp.int32, (tq, s_eff), 0)
    kpos = lax.broadcasted_iota(jnp.int32, (tq, s_eff), 1)
    score = jnp.where(kpos <= qpos, score_ref[...], -jnp.inf)

    def key_to_float(key):
        return pltpu.bitcast(jnp.where(key < 0, key ^ 0x7FFFFFFF, key), F32)

    def search(it, key):
        cand = key ^ lax.shift_left(jnp.int32(1), 31 - it)
        cnt = jnp.sum(jnp.where(score >= key_to_float(cand), 1.0, 0.0), axis=1, keepdims=True)
        return jnp.where(cnt >= topk, cand, key)

    key = lax.fori_loop(0, 32, search, jnp.full((tq, 1), INT_MIN, jnp.int32))
    thr = key_to_float(jnp.maximum(key, INT_MIN + 0x00800000))
    ge = score >= thr
    cnt = jnp.sum(jnp.where(ge, 1.0, 0.0), axis=1, keepdims=True)
    bias_ref[...] = jnp.where(ge, 0.0, -jnp.inf).astype(F32)

    @pl.when(jnp.max(cnt) > topk)
    def _():
        tw = 2 * LANES
        need = topk - jnp.sum(jnp.where(score > thr, 1.0, 0.0), axis=1, keepdims=True)
        tri = jnp.where(lax.broadcasted_iota(jnp.int32, (tw, tw), 0)
                        <= lax.broadcasted_iota(jnp.int32, (tw, tw), 1), 1.0, 0.0).astype(BF16)
        carry = jnp.zeros((tq, 1), F32)
        for c in range(s_eff // tw):
            kc = score[:, c * tw:(c + 1) * tw]
            eqc = jnp.where(kc == thr, 1.0, 0.0)
            rank = _dot(eqc.astype(BF16), tri) + carry
            take = (kc > thr) | ((kc == thr) & (rank <= need))
            bias_ref[:, c * tw:(c + 1) * tw] = jnp.where(take, 0.0, -jnp.inf).astype(F32)
            carry = carry + jnp.sum(eqc, axis=1, keepdims=True)

    bias = bias_ref[...]
    q = q_ref[...]
    ones = jnp.ones((s_eff, HEAD_DIM), BF16)
    for g in range(N_KV_HEADS):
        qg = jnp.concatenate(
            [q[:, (g * GROUP + h) * HEAD_DIM:(g * GROUP + h + 1) * HEAD_DIM] for h in range(GROUP)],
            axis=0)
        s = _dot_nt(qg, k_ref[:s_eff, g * HEAD_DIM:(g + 1) * HEAD_DIM])
        s = s.reshape(GROUP, tq, s_eff) + bias[None]
        p = jnp.exp2(s - jnp.max(s, axis=-1, keepdims=True)).astype(BF16)
        v_aug = jnp.concatenate([v_ref[:s_eff, g * HEAD_DIM:(g + 1) * HEAD_DIM], ones], axis=1)
        o = _dot(p.reshape(GROUP * tq, s_eff), v_aug)
        o = o[:, :HEAD_DIM] / o[:, HEAD_DIM:]
        for h in range(GROUP):
            col = (g * GROUP + h) * HEAD_DIM
            o_ref[:, col:col + HEAD_DIM] = o[h * tq:(h + 1) * tq].astype(o_ref.dtype)


def _dsa(qkv, qik, gw, *, batch, seq, ck=512):
    m = qkv.shape[0]
    topk = min(TOPK_MAX, seq // 4)
    kv_w = N_KV_HEADS * HEAD_DIM
    qi_w = N_IDX_HEADS * IDX_DIM
    y = jnp.zeros((m, D_MODEL), BF16)
    for n in range(1, seq // ck + 1):
        tq = ck if 2 * n * ck <= seq else ck // 2
        nq = seq // tq
        per = ck // tq
        s_eff, tile0 = n * ck, (n - 1) * per
        row = lambda b, i, tile0=tile0, nq=nq: b * nq + tile0 + i
        y = pl.pallas_call(
            functools.partial(_dsa_kernel, tq=tq, ck=ck, topk=topk, s_eff=s_eff, tile0=tile0),
            grid=(batch, per),
            in_specs=[
                pl.BlockSpec((tq, qi_w), lambda b, i, row=row: (row(b, i), 0)),
                pl.BlockSpec((seq, LANES), lambda b, i: (b, qi_w // LANES)),
                pl.BlockSpec((seq, LANES), lambda b, i: (b, qi_w // LANES + 1)),
                pl.BlockSpec((tq, LANES), lambda b, i, row=row: (row(b, i), 2 * D_MODEL // LANES)),
                pl.BlockSpec((tq, D_MODEL), lambda b, i, row=row: (row(b, i), 0)),
                pl.BlockSpec((seq, kv_w), lambda b, i: (b, D_MODEL // kv_w)),
                pl.BlockSpec((seq, kv_w), lambda b, i: (b, D_MODEL // kv_w + 1)),
                pl.BlockSpec(memory_space=pl.ANY),
            ],
            out_specs=pl.BlockSpec((tq, D_MODEL), lambda b, i, row=row: (row(b, i), 0)),
            out_shape=jax.ShapeDtypeStruct((m, D_MODEL), BF16),
            input_output_aliases={7: 0},
            scratch_shapes=[pltpu.VMEM((tq, s_eff), F32), pltpu.VMEM((tq, s_eff), F32)],
            compiler_params=pltpu.CompilerParams(
                dimension_semantics=("parallel", "arbitrary"),
                vmem_limit_bytes=VMEM_LIMIT_BYTES),
            name=f"dsa_{s_eff}",
        )(qik, qik, qik, gw, qkv, qkv, qkv, y)
    return y


def _merge_ln_kernel(ya_ref, yr_ref, ga_ref, gr_ref, wa_ref, wr_ref, wo_ref, h_ref, g_ref, b_ref, o_ref):
    j = pl.program_id(1)

    @pl.when(j == 0)
    def _():
        o_ref[...] = jnp.zeros_like(o_ref)

    merged = (jax.nn.sigmoid(ga_ref[...]) * _dot(ya_ref[...], wa_ref[...])
              + jax.nn.sigmoid(gr_ref[...]) * _dot(yr_ref[...], wr_ref[...]))
    o_ref[...] += _dot(merged.astype(BF16), wo_ref[...])

    @pl.when(j == pl.num_programs(1) - 1)
    def _():
        y = DN_ALPHA * h_ref[...] + o_ref[...]
        o_ref[...] = _layer_norm(y, g_ref[...], b_ref[...])


def _merge_ln(y_attn, y_rnn, gw, w_attn, w_rnn, w_out, h, g, b, *, tm=512, tn=512):
    m, d = h.shape
    nn = d // tn
    return pl.pallas_call(
        _merge_ln_kernel,
        grid=(m // tm, nn),
        in_specs=[
            pl.BlockSpec((tm, y_attn.shape[1]), lambda i, j: (i, 0)),
            pl.BlockSpec((tm, y_rnn.shape[1]), lambda i, j: (i, 0)),
            pl.BlockSpec((tm, tn), lambda i, j: (i, j)),
            pl.BlockSpec((tm, tn), lambda i, j: (i, j + nn)),
            pl.BlockSpec((w_attn.shape[0], tn), lambda i, j: (0, j)),
            pl.BlockSpec((w_rnn.shape[0], tn), lambda i, j: (0, j)),
            pl.BlockSpec((tn, d), lambda i, j: (j, 0)),
            pl.BlockSpec((tm, d), lambda i, j: (i, 0)),
            pl.BlockSpec((1, d), lambda i, j: (0, 0)),
            pl.BlockSpec((1, d), lambda i, j: (0, 0)),
        ],
        out_specs=pl.BlockSpec((tm, d), lambda i, j: (i, 0)),
        out_shape=jax.ShapeDtypeStruct((m, d), F32),
        compiler_params=pltpu.CompilerParams(
            dimension_semantics=("parallel", "arbitrary"),
            vmem_limit_bytes=VMEM_LIMIT_BYTES),
        name="merge_ln",
    )(y_attn, y_rnn, gw, gw, w_attn, w_rnn, w_out, h, g, b)


def _block_diag_groups(w):
    per = LRU_BLOCKS // LRU_GROUPS
    w4 = w.reshape(LRU_GROUPS, per, LRU_BLOCK, LRU_BLOCK)
    bd = jnp.einsum("gaij,ab->gaibj", w4, jnp.eye(per, dtype=w.dtype))
    return bd.reshape(LRU_GROUPS, LRU_GROUP_W, LRU_GROUP_W)


def kernel(x, positions, ffn1_w_in, ffn1_w_out, ln1_g, ln1_b, w_in, conv_w, conv_b, lru_wa, lru_ba,
           lru_wx, lru_bx, lru_lambda, w_attn_branch, w_rnn_branch, w_out, ln2_g, ln2_b, ffn2_w_in,
           ffn2_w_out, ln3_g, ln3_b):
    batch, seq, d = x.shape
    m = batch * seq
    h = x.reshape(m, d)
    pos_col = positions.reshape(m, 1)
    tab32, tab16 = _rope_tables(pos_col)

    for l in range(DEPTH):
        h, hb = _ffn_ln(h, ffn1_w_in[l].astype(BF16), ffn1_w_out[l].astype(BF16), ln1_g[l][None],
                        ln1_b[l][None], emit_bf16=True)

        wl = w_in[l]
        o_q, o_k, o_v = 0, D_MODEL, D_MODEL + N_KV_HEADS * HEAD_DIM
        o_qi = o_v + N_KV_HEADS * HEAD_DIM
        o_ki = o_qi + N_IDX_HEADS * IDX_DIM
        o_wi = o_ki + IDX_DIM
        o_rx = o_wi + N_IDX_HEADS
        o_gt = o_rx + 2 * D_RNN
        zk = jnp.zeros((d, IDX_DIM), wl.dtype)
        w_ki = wl[:, o_ki:o_wi]
        w_qkv = wl[:, o_q:o_qi].astype(BF16)
        w_qik = jnp.concatenate([wl[:, o_qi:o_ki], w_ki, zk, zk, w_ki], axis=1).astype(BF16)
        w_rxg = wl[:, o_rx:o_gt].astype(BF16)
        pad = jnp.zeros((d, 512 - N_IDX_HEADS), wl.dtype)
        w_gw = jnp.concatenate([wl[:, o_gt:], wl[:, o_wi:o_rx], pad], axis=1).astype(BF16)

        qkv = _proj(hb, w_qkv, BF16, tm=1024, tn=w_qkv.shape[1], tab=tab32, rope_shift=ROPE_DIM // 2,
                    n_rope_tiles=(D_MODEL + N_KV_HEADS * HEAD_DIM) // LANES,
                    n_scale_tiles=D_MODEL // LANES, scale=HEAD_DIM ** -0.5 * LOG2_E, name="proj_qkv")
        qik = _proj(hb, w_qik, BF16, tm=1024, tn=w_qik.shape[1], tab=tab16, rope_shift=IDX_ROPE_DIM // 2,
                    n_rope_tiles=w_qik.shape[1] // LANES, name="proj_idx")
        rxg = _proj(hb, w_rxg, F32, tm=1024, tn=1024, name="proj_rnn")
        gw = _proj(hb, w_gw, F32, tm=1024, tn=1536, name="proj_gate")

        y_rnn = _rglru(rxg, conv_w[l], conv_b[l][None],
                       _block_diag_groups(lru_wa[l]).astype(BF16), _block_diag_groups(lru_wx[l]).astype(BF16),
                       lru_ba[l][None], lru_bx[l][None], lru_lambda[l][None], batch=batch, seq=seq)
        y_attn = _dsa(qkv, qik, gw, batch=batch, seq=seq)

        h = _merge_ln(y_attn, y_rnn, gw, w_attn_branch[l].astype(BF16), w_rnn_branch[l].astype(BF16),
                      w_out[l].astype(BF16), h, ln2_g[l][None], ln2_b[l][None])
        h = _ffn_ln(h, ffn2_w_in[l].astype(BF16), ffn2_w_out[l].astype(BF16), ln3_g[l][None], ln3_b[l][None],
                    emit_bf16=False)
    return h.reshape(batch, seq, d)
```
